```python
import math
import jax, jax.numpy as jnp
from jax import lax
import numpy as np

D_MODEL = 2048
BATCH = 8
SEQ = 2048
DEPTH = 2

HEAD_DIM = 128
MIX_WIDTH = D_MODEL
GDN_HEADS = MIX_WIDTH // (2 * HEAD_DIM)
GDN_WIDTH = GDN_HEADS * HEAD_DIM
LRU_WIDTH = MIX_WIDTH - GDN_WIDTH
LRU_BLOCK = HEAD_DIM
LRU_BLOCKS = LRU_WIDTH // LRU_BLOCK
FOX_HEADS = MIX_WIDTH // (2 * HEAD_DIM)
FOX_WIDTH = FOX_HEADS * HEAD_DIM
MOBA_HEADS = (MIX_WIDTH - FOX_WIDTH) // HEAD_DIM
MOBA_WIDTH = MOBA_HEADS * HEAD_DIM
CONV_WIDTH = 4
GDN_CHUNK = 64
LRU_C = 8.0
FOX_Q_BLOCK = 128
MOBA_BLOCK = 256
MOBA_TOPK = 3
MOBA_Q_CHUNK = 16
ROPE_THETA = 500000.0
ROPE_DIM = HEAD_DIM // 4
D_FF = ((8 * D_MODEL + 3 * 256 - 1) // (3 * 256)) * 256
NORM_EPS = 1e-6
N_EVEN = (DEPTH + 1) // 2
N_ODD = DEPTH // 2
AB_SIZES = (GDN_WIDTH, GDN_WIDTH, GDN_WIDTH, GDN_WIDTH, GDN_HEADS, GDN_HEADS, LRU_WIDTH, LRU_WIDTH)
AB_IN = 4 * GDN_WIDTH + 2 * GDN_HEADS + 2 * LRU_WIDTH
CD_SIZES = (FOX_WIDTH, FOX_WIDTH, FOX_WIDTH, FOX_HEADS, MOBA_WIDTH, MOBA_WIDTH, MOBA_WIDTH)
CD_IN = 3 * FOX_WIDTH + FOX_HEADS + 3 * MOBA_WIDTH

kernel_name = 'hybrid_gdn_rglru_fox_moba_block'


def rmsnorm(x, w):
    xf = x.astype(jnp.float32)
    y = xf * lax.rsqrt(jnp.mean(xf * xf, axis=-1, keepdims=True) + NORM_EPS)
    return (y * w.astype(jnp.float32)).astype(x.dtype)


def l2norm(x):
    return x * lax.rsqrt(jnp.sum(x * x, axis=-1, keepdims=True) + NORM_EPS)


def split_cols(t, sizes):
    outs, start = [], 0
    for n in sizes:
        outs.append(t[..., start:start + n])
        start += n
    return outs


def to_heads(t, n_heads):
    b, s, _ = t.shape
    return t.reshape(b, s, n_heads, -1).transpose(0, 2, 1, 3)


def from_heads(t):
    b, h, s, d = t.shape
    return t.transpose(0, 2, 1, 3).reshape(b, s, h * d)


def causal_dwconv(x, w):
    c = x.shape[-1]
    return lax.conv_general_dilated(
        x, w[:, None, :].astype(x.dtype), window_strides=(1,),
        padding=[(w.shape[0] - 1, 0)], dimension_numbers=('NWC', 'WIO', 'NWC'),
        feature_group_count=c)


def partial_rotary(x):
    s = x.shape[2]
    half = ROPE_DIM // 2
    inv_freq = ROPE_THETA ** (-jnp.arange(half, dtype=jnp.float32) / half)
    ang = jnp.arange(s, dtype=jnp.float32)[:, None] * inv_freq[None, :]
    cos, sin = jnp.cos(ang), jnp.sin(ang)
    xf = x.astype(jnp.float32)
    x1, x2 = xf[..., :half], xf[..., half:ROPE_DIM]
    out = jnp.concatenate([x1 * cos - x2 * sin, x2 * cos + x1 * sin, xf[..., ROPE_DIM:]], axis=-1)
    return out.astype(x.dtype)


def gated_delta_rule_chunked(q, k, v, g, beta):
    b, h, s, dk = q.shape
    dv = v.shape[-1]
    c = GDN_CHUNK
    n = s // c
    q = (q * dk ** -0.5).reshape(b, h, n, c, dk)
    k = k.reshape(b, h, n, c, dk)
    v = v.reshape(b, h, n, c, dv)
    beta = beta.reshape(b, h, n, c)
    g = jnp.cumsum(g.reshape(b, h, n, c), axis=-1)
    incl = jnp.tril(jnp.ones((c, c), bool))
    strict = jnp.tril(jnp.ones((c, c), bool), -1)
    decay = jnp.where(incl, jnp.exp(jnp.where(incl, g[..., :, None] - g[..., None, :], 0.0)), 0.0)
    k_beta = k * beta[..., None]
    m = jnp.where(strict, jnp.einsum('bhncd,bhnsd->bhncs', k_beta, k) * decay, 0.0)
    eye = jnp.eye(c, dtype=m.dtype)
    t_inv = lax.linalg.triangular_solve(eye + m, jnp.broadcast_to(eye, m.shape),
                                        left_side=True, lower=True, unit_diagonal=True)
    u = jnp.einsum('bhncs,bhnse->bhnce', t_inv, v * beta[..., None])
    w = jnp.einsum('bhncs,bhnsd->bhncd', t_inv, k_beta * jnp.exp(g)[..., None])
    attn = jnp.where(incl, jnp.einsum('bhncd,bhnsd->bhncs', q, k) * decay, 0.0)
    q_dec = q * jnp.exp(g)[..., None]
    k_dec = k * jnp.exp(g[..., -1:] - g)[..., None]
    g_tot = jnp.exp(g[..., -1])

    def step(state, xs):
        u_i, w_i, q_i, a_i, k_i, gt_i = xs
        v_new = u_i - jnp.einsum('bhcd,bhde->bhce', w_i, state)
        o_i = jnp.einsum('bhcd,bhde->bhce', q_i, state) + jnp.einsum('bhcs,bhse->bhce', a_i, v_new)
        state = state * gt_i[..., None, None] + jnp.einsum('bhcd,bhce->bhde', k_i, v_new)
        return state, o_i

    xs = tuple(jnp.moveaxis(t, 2, 0) for t in (u, w, q_dec, attn, k_dec, g_tot))
    state0 = jnp.zeros((b, h, dk, dv), q.dtype)
    _, o = lax.scan(step, state0, xs)
    return jnp.moveaxis(o, 0, 2).reshape(b, h, s, dv)


def gated_deltanet(q, k, v, z, b_logit, a_logit, conv_w, a_log, dt_bias, out_norm_w):
    dtype = q.dtype
    bsz, s, _ = q.shape
    qkv = jax.nn.silu(causal_dwconv(jnp.concatenate([q, k, v], axis=-1), conv_w))
    q, k, v = split_cols(qkv, (GDN_WIDTH, GDN_WIDTH, GDN_WIDTH))
    q = l2norm(to_heads(q, GDN_HEADS).astype(jnp.float32))
    k = l2norm(to_heads(k, GDN_HEADS).astype(jnp.float32))
    v = to_heads(v, GDN_HEADS).astype(jnp.float32)
    beta = jax.nn.sigmoid(b_logit.astype(jnp.float32)).transpose(0, 2, 1)
    g = -(jnp.exp(a_log.astype(jnp.float32))
          * jax.nn.softplus(a_logit.astype(jnp.float32) + dt_bias.astype(jnp.float32)))
    g = g.transpose(0, 2, 1)
    o = gated_delta_rule_chunked(q, k, v, g, beta).transpose(0, 2, 1, 3)
    zh = z.reshape(bsz, s, GDN_HEADS, HEAD_DIM).astype(jnp.float32)
    o = rmsnorm(o, out_norm_w) * jax.nn.silu(zh)
    return o.reshape(bsz, s, GDN_WIDTH).astype(dtype)


def rglru_branch(xb, yb, conv_w, conv_b, wa, ba, wx, bx, lam):
    bsz, s, wdt = xb.shape
    xc = causal_dwconv(xb, conv_w) + conv_b
    xblk = xc.reshape(bsz, s, LRU_BLOCKS, LRU_BLOCK)
    r = jax.nn.sigmoid((jnp.einsum('bsnc,ncd->bsnd', xblk, wa).reshape(bsz, s, wdt) + ba).astype(jnp.float32))
    i = jax.nn.sigmoid((jnp.einsum('bsnc,ncd->bsnd', xblk, wx).reshape(bsz, s, wdt) + bx).astype(jnp.float32))
    log_a = -LRU_C * r * jax.nn.softplus(-lam.astype(jnp.float32))
    a = jnp.exp(log_a)
    inp = jnp.sqrt(-jnp.expm1(2.0 * log_a)) * i * xc.astype(jnp.float32)

    def combine(left, right):
        a_l, b_l = left
        a_r, b_r = right
        return a_l * a_r, a_r * b_l + b_r

    _, hseq = lax.associative_scan(combine, (a, inp), axis=1)
    return (hseq * jax.nn.gelu(yb.astype(jnp.float32))).astype(xb.dtype)


def forgetting_attention(q, k, v, log_f):
    b, h, s, d = q.shape
    nq = s // FOX_Q_BLOCK
    cum_f = jnp.cumsum(log_f, axis=-1)
    q_blocks = jnp.moveaxis(q.reshape(b, h, nq, FOX_Q_BLOCK, d), 2, 0)
    f_blocks = jnp.moveaxis(cum_f.reshape(b, h, nq, FOX_Q_BLOCK), 2, 0)
    k_pos = jnp.arange(s)
    scale = d ** -0.5

    def block(args):
        i, q_i, f_i = args
        sc = jnp.einsum('bhqd,bhkd->bhqk', q_i, k).astype(jnp.float32) * scale
        sc = sc + f_i[..., :, None] - cum_f[..., None, :]
        q_pos = i * FOX_Q_BLOCK + jnp.arange(FOX_Q_BLOCK)
        sc = jnp.where(k_pos[None, :] <= q_pos[:, None], sc, -jnp.inf)
        p = jax.nn.softmax(sc, axis=-1).astype(v.dtype)
        return jnp.einsum('bhqk,bhkd->bhqd', p, v)

    o = lax.map(block, (jnp.arange(nq), q_blocks, f_blocks))
    return jnp.moveaxis(o, 0, 2).reshape(b, h, s, d)


def moba_attention(q, k, v):
    b, h, s, d = q.shape
    nb = -(-s // MOBA_BLOCK)
    sp = nb * MOBA_BLOCK
    pad = ((0, 0), (0, 0), (0, sp - s), (0, 0))
    q, k, v = jnp.pad(q, pad), jnp.pad(k, pad), jnp.pad(v, pad)
    kb = k.reshape(b, h, nb, MOBA_BLOCK, d)
    vb = v.reshape(b, h, nb, MOBA_BLOCK, d)
    k_mean = jnp.mean(kb.astype(jnp.float32), axis=3)
    gate = jnp.einsum('bhtd,bhnd->bhtn', q.astype(jnp.float32), k_mean)
    t_blk = jnp.arange(sp) // MOBA_BLOCK
    past = jnp.arange(nb)[None, :] < t_blk[:, None]
    gate = jnp.where(past, gate, -jnp.inf)
    topk = min(MOBA_TOPK, nb)
    _, sel = lax.top_k(gate, topk)
    valid = jnp.arange(topk)[None, :] < t_blk[:, None]
    nc = sp // MOBA_Q_CHUNK
    q_c = jnp.moveaxis(q.reshape(b, h, nc, MOBA_Q_CHUNK, d), 2, 0)
    sel_c = jnp.moveaxis(sel.reshape(b, h, nc, MOBA_Q_CHUNK, topk), 2, 0)
    valid_c = valid.reshape(nc, MOBA_Q_CHUNK, topk)
    b_ix = jnp.arange(b)[:, None, None, None]
    h_ix = jnp.arange(h)[None, :, None, None]
    scale = d ** -0.5

    def chunk(args):
        i, q_i, sel_i, valid_i = args
        q_pos = i * MOBA_Q_CHUNK + jnp.arange(MOBA_Q_CHUNK)
        own = (i * MOBA_Q_CHUNK) // MOBA_BLOCK
        k_own = lax.dynamic_index_in_dim(kb, own, axis=2, keepdims=False)
        v_own = lax.dynamic_index_in_dim(vb, own, axis=2, keepdims=False)
        s_own = jnp.einsum('bhqd,bhkd->bhqk', q_i, k_own).astype(jnp.float32) * scale
        k_pos = own * MOBA_BLOCK + jnp.arange(MOBA_BLOCK)
        s_own = jnp.where(k_pos[None, :] <= q_pos[:, None], s_own, -jnp.inf)
        k_sel = kb[b_ix, h_ix, sel_i]
        v_sel = vb[b_ix, h_ix, sel_i]
        s_sel = jnp.einsum('bhqd,bhqnkd->bhqnk', q_i, k_sel).astype(jnp.float32) * scale
        s_sel = jnp.where(valid_i[None, None, :, :, None], s_sel, -jnp.inf)
        s_all = jnp.concatenate([s_own, s_sel.reshape(b, h, MOBA_Q_CHUNK, topk * MOBA_BLOCK)], axis=-1)
        p = jax.nn.softmax(s_all, axis=-1).astype(v.dtype)
        p_own = p[..., :MOBA_BLOCK]
        p_sel = p[..., MOBA_BLOCK:].reshape(b, h, MOBA_Q_CHUNK, topk, MOBA_BLOCK)
        return (jnp.einsum('bhqk,bhkd->bhqd', p_own, v_own)
                + jnp.einsum('bhqnk,bhqnkd->bhqd', p_sel, v_sel))

    o = lax.map(chunk, (jnp.arange(nc), q_c, sel_c, valid_c))
    return jnp.moveaxis(o, 0, 2).reshape(b, h, sp, d)[:, :, :s]


def gdn_rglru_mixer(hn, w_in, conv_qkv, a_log, dt_bias, out_norm_w,
                    lru_conv_w, lru_conv_b, lru_wa, lru_ba, lru_wx, lru_bx, lru_lambda, w_out):
    q, k, v, z, b_logit, a_logit, lx, ly = split_cols(hn @ w_in, AB_SIZES)
    o_a = gated_deltanet(q, k, v, z, b_logit, a_logit, conv_qkv, a_log, dt_bias, out_norm_w)
    o_b = rglru_branch(lx, ly, lru_conv_w, lru_conv_b, lru_wa, lru_ba, lru_wx, lru_bx, lru_lambda)
    return jnp.concatenate([o_a, o_b], axis=-1) @ w_out


def fox_moba_mixer(hn, w_in, f_bias, w_out):
    fq, fk, fv, ff, mq, mk, mv = split_cols(hn @ w_in, CD_SIZES)
    log_f = jax.nn.log_sigmoid(ff.astype(jnp.float32) + f_bias.astype(jnp.float32)).transpose(0, 2, 1)
    o_c = forgetting_attention(to_heads(fq, FOX_HEADS), to_heads(fk, FOX_HEADS), to_heads(fv, FOX_HEADS), log_f)
    o_d = moba_attention(partial_rotary(to_heads(mq, MOBA_HEADS)), partial_rotary(to_heads(mk, MOBA_HEADS)),
                         to_heads(mv, MOBA_HEADS))
    return jnp.concatenate([from_heads(o_c), from_heads(o_d)], axis=-1) @ w_out


def swiglu(hn, w_gate, w_up, w_down):
    return (jax.nn.silu(hn @ w_gate) * (hn @ w_up)) @ w_down


def setup_inputs(seed: int = 0) -> dict:
    key = jax.random.key(seed)
    ks = iter(jax.random.split(key, 32))
    f32 = jnp.float32

    def nrm(shape, scale):
        return jax.random.normal(next(ks), shape, f32) * scale

    x = nrm((BATCH, SEQ, D_MODEL), 1.0)
    ab_norm = 1.0 + nrm((N_EVEN, D_MODEL), 0.01)
    ab_w_in = nrm((N_EVEN, D_MODEL, AB_IN), D_MODEL ** -0.5)
    ab_conv_qkv = nrm((N_EVEN, CONV_WIDTH, 3 * GDN_WIDTH), CONV_WIDTH ** -0.5)
    ab_a_log = jnp.log(jax.random.uniform(next(ks), (N_EVEN, GDN_HEADS), f32, 1.0, 16.0))
    dt = jnp.exp(jax.random.uniform(next(ks), (N_EVEN, GDN_HEADS), f32, math.log(1e-3), math.log(1e-1)))
    ab_dt_bias = dt + jnp.log(-jnp.expm1(-dt))
    ab_out_norm = 1.0 + nrm((N_EVEN, HEAD_DIM), 0.01)
    ab_lru_conv_w = nrm((N_EVEN, CONV_WIDTH, LRU_WIDTH), CONV_WIDTH ** -0.5)
    ab_lru_conv_b = nrm((N_EVEN, LRU_WIDTH), 0.01)
    ab_lru_wa = nrm((N_EVEN, LRU_BLOCKS, LRU_BLOCK, LRU_BLOCK), LRU_BLOCK ** -0.5)
    ab_lru_ba = nrm((N_EVEN, LRU_WIDTH), 0.01)
    ab_lru_wx = nrm((N_EVEN, LRU_BLOCKS, LRU_BLOCK, LRU_BLOCK), LRU_BLOCK ** -0.5)
    ab_lru_bx = nrm((N_EVEN, LRU_WIDTH), 0.01)
    a0 = jax.random.uniform(next(ks), (N_EVEN, LRU_WIDTH), f32, 0.9, 0.999) ** (1.0 / LRU_C)
    ab_lru_lambda = jnp.log(a0) - jnp.log1p(-a0)
    ab_w_out = nrm((N_EVEN, MIX_WIDTH, D_MODEL), MIX_WIDTH ** -0.5)
    cd_norm = 1.0 + nrm((N_ODD, D_MODEL), 0.01)
    cd_w_in = nrm((N_ODD, D_MODEL, CD_IN), D_MODEL ** -0.5)
    cd_f_bias = jax.random.uniform(next(ks), (N_ODD, FOX_HEADS), f32, 1.0, 4.0)
    cd_w_out = nrm((N_ODD, MIX_WIDTH, D_MODEL), MIX_WIDTH ** -0.5)
    ffn_norm = 1.0 + nrm((DEPTH, D_MODEL), 0.01)
    ffn_w_gate = nrm((DEPTH, D_MODEL, D_FF), D_MODEL ** -0.5)
    ffn_w_up = nrm((DEPTH, D_MODEL, D_FF), D_MODEL ** -0.5)
    ffn_w_down = nrm((DEPTH, D_FF, D_MODEL), D_FF ** -0.5)
    final_norm = 1.0 + nrm((D_MODEL,), 0.01)
    return {'x': x, 'ab_norm': ab_norm, 'ab_w_in': ab_w_in, 'ab_conv_qkv': ab_conv_qkv,
            'ab_a_log': ab_a_log, 'ab_dt_bias': ab_dt_bias, 'ab_out_norm': ab_out_norm,
            'ab_lru_conv_w': ab_lru_conv_w, 'ab_lru_conv_b': ab_lru_conv_b,
            'ab_lru_wa': ab_lru_wa, 'ab_lru_ba': ab_lru_ba, 'ab_lru_wx': ab_lru_wx, 'ab_lru_bx': ab_lru_bx,
            'ab_lru_lambda': ab_lru_lambda, 'ab_w_out': ab_w_out,
            'cd_norm': cd_norm, 'cd_w_in': cd_w_in, 'cd_f_bias': cd_f_bias, 'cd_w_out': cd_w_out,
            'ffn_norm': ffn_norm, 'ffn_w_gate': ffn_w_gate, 'ffn_w_up': ffn_w_up, 'ffn_w_down': ffn_w_down,
            'final_norm': final_norm}


def reference(x, ab_norm, ab_w_in, ab_conv_qkv, ab_a_log, ab_dt_bias, ab_out_norm,
              ab_lru_conv_w, ab_lru_conv_b, ab_lru_wa, ab_lru_ba, ab_lru_wx, ab_lru_bx,
              ab_lru_lambda, ab_w_out, cd_norm, cd_w_in, cd_f_bias, cd_w_out,
              ffn_norm, ffn_w_gate, ffn_w_up, ffn_w_down, final_norm):
    h = x
    for layer in range(DEPTH):
        j = layer // 2
        if layer % 2 == 0:
            h = h + gdn_rglru_mixer(rmsnorm(h, ab_norm[j]), ab_w_in[j], ab_conv_qkv[j], ab_a_log[j],
                                    ab_dt_bias[j], ab_out_norm[j], ab_lru_conv_w[j], ab_lru_conv_b[j],
                                    ab_lru_wa[j], ab_lru_ba[j], ab_lru_wx[j], ab_lru_bx[j],
                                    ab_lru_lambda[j], ab_w_out[j])
        else:
            h = h + fox_moba_mixer(rmsnorm(h, cd_norm[j]), cd_w_in[j], cd_f_bias[j], cd_w_out[j])
        h = h + swiglu(rmsnorm(h, ffn_norm[layer]), ffn_w_gate[layer], ffn_w_up[layer], ffn_w_down[layer])
    return rmsnorm(h, final_norm)
```

```python
import functools
import math

import jax
import jax.numpy as jnp
from jax import lax
from jax.experimental import pallas as pl
from jax.experimental.pallas import tpu as pltpu

F32 = jnp.float32
BF16 = jnp.bfloat16

HEAD_DIM = 128
LANES = 128
NORM_EPS = 1e-6
CONV_WIDTH = 4
GDN_CHUNK = 64
LRU_C = 8.0
MOBA_BLOCK = 256
MOBA_TOPK = 3
ROPE_THETA = 500000.0
ROPE_DIM = HEAD_DIM // 4
ATTN_Q_TILE = 256
VMEM_LIMIT_BYTES = 48 * 1024 * 1024


def _cparams(n_axes):
    return pltpu.CompilerParams(dimension_semantics=("arbitrary",) * n_axes,
                                vmem_limit_bytes=VMEM_LIMIT_BYTES)


def _dot(a, b):
    return jnp.dot(a, b, preferred_element_type=F32)


def _dot_nt(a, b):
    return lax.dot_general(a, b, (((1,), (1,)), ((), ())), preferred_element_type=F32)


def _dot_tn(a, b):
    return lax.dot_general(a, b, (((0,), (0,)), ((), ())), preferred_element_type=F32)


def _dot_f32(a, b):
    return jnp.dot(a, b, preferred_element_type=F32, precision=lax.Precision.HIGHEST)


def _rmsnorm(x, w):
    ms = jnp.mean(x * x, axis=-1, keepdims=True)
    return x * lax.rsqrt(ms + NORM_EPS) * w


def _sigmoid(x):
    return 1.0 / (1.0 + jnp.exp(-x))


def _softplus(x):
    return jnp.maximum(x, 0.0) + jnp.log1p(jnp.exp(-jnp.abs(x)))


def _silu(x):
    return x * _sigmoid(x)


def _causal_conv(x, w):
    row = lax.broadcasted_iota(jnp.int32, x.shape, 0)
    k = w.shape[0]
    acc = x * w[k - 1:k, :]
    for j in range(k - 1):
        sh = k - 1 - j
        acc = acc + jnp.where(row >= sh, pltpu.roll(x, sh, 0), 0.0) * w[j:j + 1, :]
    return acc


def _lane_column(x, idx):
    lane = lax.broadcasted_iota(jnp.int32, x.shape, 1)
    return jnp.sum(jnp.where(lane == idx, x, 0.0), axis=-1, keepdims=True)


def _lane_cumsum(x, period):
    pos = lax.broadcasted_iota(jnp.int32, x.shape, 1) % period
    sh = 1
    while sh < period:
        x = x + jnp.where(pos >= sh, pltpu.roll(x, sh, 1), 0.0)
        sh *= 2
    return x


def _rows_to_columns(rows8, n_lanes):
    pad = jnp.zeros((n_lanes - rows8.shape[0], rows8.shape[1]), F32)
    return jnp.concatenate([rows8, pad], axis=0).T


def _inproj_kernel(x_ref, nw_ref, w_ref, wg_ref, o_ref, og_ref, xn_ref):
    @pl.when(pl.program_id(1) == 0)
    def _():
        xn = _rmsnorm(x_ref[...], nw_ref[...]).astype(BF16)
        xn_ref[...] = xn
        og_ref[...] = _dot(xn, wg_ref[...])

    o_ref[...] = _dot(xn_ref[...], w_ref[...])


def _inproj(x, nw, w, wg, *, tm=1024, tn=512):
    t, d = x.shape
    n = w.shape[1]
    return pl.pallas_call(
        _inproj_kernel,
        grid=(t // tm, n // tn),
        in_specs=[pl.BlockSpec((tm, d), lambda i, j: (i, 0)),
                  pl.BlockSpec((1, d), lambda i, j: (0, 0)),
                  pl.BlockSpec((d, tn), lambda i, j: (0, j)),
                  pl.BlockSpec((d, LANES), lambda i, j: (0, 0))],
        out_specs=[pl.BlockSpec((tm, tn), lambda i, j: (i, j)),
                   pl.BlockSpec((tm, LANES), lambda i, j: (i, 0))],
        out_shape=[jax.ShapeDtypeStruct((t, n), F32), jax.ShapeDtypeStruct((t, LANES), F32)],
        scratch_shapes=[pltpu.VMEM((tm, d), BF16)],
        compiler_params=_cparams(2),
        name="inproj",
    )(x, nw.reshape(1, d), w, wg)


def _outproj_kernel(res_ref, a1_ref, a2_ref, w1_ref, w2_ref, o_ref):
    o_ref[...] = res_ref[...] + _dot(a1_ref[...], w1_ref[...]) + _dot(a2_ref[...], w2_ref[...])


def _outproj(res, a1, a2, w1, w2, *, tm=1024, tn=512):
    t, d = res.shape
    k1, k2 = a1.shape[1], a2.shape[1]
    return pl.pallas_call(
        _outproj_kernel,
        grid=(t // tm, d // tn),
        in_specs=[pl.BlockSpec((tm, tn), lambda i, j: (i, j)),
                  pl.BlockSpec((tm, k1), lambda i, j: (i, 0)),
                  pl.BlockSpec((tm, k2), lambda i, j: (i, 0)),
                  pl.BlockSpec((k1, tn), lambda i, j: (0, j)),
                  pl.BlockSpec((k2, tn), lambda i, j: (0, j))],
        out_specs=pl.BlockSpec((tm, tn), lambda i, j: (i, j)),
        out_shape=jax.ShapeDtypeStruct((t, d), F32),
        compiler_params=_cparams(2),
        name="outproj",
    )(res, a1, a2, w1, w2)


def _ffn_kernel(x_ref, nw_ref, wg_ref, wu_ref, wd_ref, fnw_ref, o_ref, xn_ref, acc_ref, *, final_norm):
    f = pl.program_id(1)

    @pl.when(f == 0)
    def _():
        xn_ref[...] = _rmsnorm(x_ref[...], nw_ref[...]).astype(BF16)
        acc_ref[...] = jnp.zeros_like(acc_ref)

    xn = xn_ref[...]
    gate = _dot(xn, wg_ref[...])
    up = _dot(xn, wu_ref[...])
    acc_ref[...] += _dot((_silu(gate) * up).astype(BF16), wd_ref[...])

    @pl.when(f == pl.num_programs(1) - 1)
    def _():
        h = x_ref[...] + acc_ref[...]
        if final_norm:
            h = _rmsnorm(h, fnw_ref[...])
        o_ref[...] = h


def _ffn(x, nw, wg, wu, wd, fnw, *, final_norm, tm=512, tf=512):
    t, d = x.shape
    dff = wg.shape[1]
    return pl.pallas_call(
        functools.partial(_ffn_kernel, final_norm=final_norm),
        grid=(t // tm, dff // tf),
        in_specs=[pl.BlockSpec((tm, d), lambda i, f: (i, 0)),
                  pl.BlockSpec((1, d), lambda i, f: (0, 0)),
                  pl.BlockSpec((d, tf), lambda i, f: (0, f)),
                  pl.BlockSpec((d, tf), lambda i, f: (0, f)),
                  pl.BlockSpec((tf, d), lambda i, f: (f, 0)),
                  pl.BlockSpec((1, d), lambda i, f: (0, 0))],
        out_specs=pl.BlockSpec((tm, d), lambda i, f: (i, 0)),
        out_shape=jax.ShapeDtypeStruct((t, d), F32),
        scratch_shapes=[pltpu.VMEM((tm, d), BF16), pltpu.VMEM((tm, d), F32)],
        compiler_params=_cparams(2),
        name="ffn",
    )(x, nw.reshape(1, d), wg, wu, wd, fnw.reshape(1, d))


def _unit_lower_inverse(m):
    c = m.shape[0]
    eye = (lax.broadcasted_iota(jnp.int32, (c, c), 0) == lax.broadcasted_iota(jnp.int32, (c, c), 1)).astype(F32)
    p = -m
    inv = eye + p
    span = 2
    while span < c:
        p = _dot_f32(p, p)
        inv = inv + _dot_f32(inv, p)
        span *= 2
    return inv


def _gdn_kernel(q_ref, k_ref, v_ref, z_ref, g_ref, cq_ref, ck_ref, cv_ref, alog_ref, dtb_ref, onw_ref,
                o_ref, beta_s, gcc_s, gcr_s, q_s, k_s, v_s, bc_s, gc_s, out_s, *, n_heads):
    h = pl.program_id(1)
    s_len = q_ref.shape[0]
    c = GDN_CHUNK

    @pl.when(h == 0)
    def _():
        logits = g_ref[...]
        beta_s[...] = _sigmoid(logits)
        log_decay = -jnp.exp(alog_ref[...]) * _softplus(logits + dtb_ref[...])
        rows = log_decay.T[n_heads:2 * n_heads, :]
        rows = _lane_cumsum(rows, c)
        for r in range(n_heads):
            gcr_s[r] = rows[r:r + 1, :]
        gcc_s[...] = _rows_to_columns(rows, LANES)

    bc_s[...] = jnp.broadcast_to(_lane_column(beta_s[...], h), bc_s.shape)
    gc_s[...] = jnp.broadcast_to(_lane_column(gcc_s[...], h), gc_s.shape)
    q = _silu(_causal_conv(q_ref[...], cq_ref[...]))
    k = _silu(_causal_conv(k_ref[...], ck_ref[...]))
    v_s[...] = _silu(_causal_conv(v_ref[...], cv_ref[...]))
    q_s[...] = q * lax.rsqrt(jnp.sum(q * q, axis=-1, keepdims=True) + NORM_EPS) * (HEAD_DIM ** -0.5)
    k_s[...] = k * lax.rsqrt(jnp.sum(k * k, axis=-1, keepdims=True) + NORM_EPS)

    ri = lax.broadcasted_iota(jnp.int32, (c, c), 0)
    ci = lax.broadcasted_iota(jnp.int32, (c, c), 1)
    incl = ri >= ci
    strict = ri > ci

    def pair_body(pi, state):
        gc_row_pair = gcr_s[h, :, pl.ds(pl.multiple_of(pi * (2 * c), 2 * c), 2 * c)]
        for half in range(2):
            r0 = pl.multiple_of(pi * (2 * c) + half * c, c)
            rows = pl.ds(r0, c)
            qc, kc, vc = q_s[rows, :], k_s[rows, :], v_s[rows, :]
            beta, gc = bc_s[rows, :], gc_s[rows, :]
            gc_row = gc_row_pair[:, half * c:(half + 1) * c]
            diff = gc[:, :c] - gc_row
            decay = jnp.where(incl, jnp.exp(jnp.where(incl, diff, 0.0)), 0.0)
            k_beta = kc * beta
            kb16, k16 = k_beta.astype(BF16), kc.astype(BF16)
            m = jnp.where(strict, _dot_nt(kb16, k16) * decay, 0.0)
            t_inv = _unit_lower_inverse(m)
            t16 = t_inv.astype(BF16)
            e_gc = jnp.exp(gc)
            u = _dot(t16, (vc * beta).astype(BF16))
            w = _dot(t16, (k_beta * e_gc).astype(BF16))
            attn = jnp.where(incl, _dot_nt(qc.astype(BF16), k16) * decay, 0.0)
            q_dec = qc * e_gc
            gc_last = gc[c - 1:c, :]
            k_dec = kc * jnp.exp(gc_last - gc)
            s16 = state.astype(BF16)
            v_new = u - _dot(w.astype(BF16), s16)
            vn16 = v_new.astype(BF16)
            out_s[rows, :] = _dot(q_dec.astype(BF16), s16) + _dot(attn.astype(BF16), vn16)
            state = state * jnp.exp(gc_last) + _dot_tn(k_dec.astype(BF16), vn16)
        return state

    lax.fori_loop(0, s_len // (2 * c), pair_body, jnp.zeros((HEAD_DIM, HEAD_DIM), F32))

    o = _rmsnorm(out_s[...], onw_ref[...]) * _silu(z_ref[...])
    o_ref[...] = o.astype(o_ref.dtype)


def _gdn(proj, gates, conv_w, alog_pad, dtb_pad, out_norm_w, *, batch, n_heads):
    t = proj.shape[0]
    s_len = t // batch
    hd = HEAD_DIM
    seq = lambda off: pl.BlockSpec((s_len, hd), lambda b, h: (b, off + h))
    cw = lambda off: pl.BlockSpec((CONV_WIDTH, hd), lambda b, h: (0, off + h))
    row = pl.BlockSpec((1, LANES), lambda b, h: (0, 0))
    act = pltpu.VMEM((s_len, hd), F32)
    return pl.pallas_call(
        functools.partial(_gdn_kernel, n_heads=n_heads),
        grid=(batch, n_heads),
        in_specs=[seq(0), seq(n_heads), seq(2 * n_heads), seq(3 * n_heads),
                  pl.BlockSpec((s_len, LANES), lambda b, h: (b, 0)),
                  cw(0), cw(n_heads), cw(2 * n_heads), row, row, row],
        out_specs=pl.BlockSpec((s_len, hd), lambda b, h: (b, h)),
        out_shape=jax.ShapeDtypeStruct((t, n_heads * hd), BF16),
        scratch_shapes=[act, act, pltpu.VMEM((n_heads, 1, s_len), F32), act, act, act, act, act, act],
        compiler_params=_cparams(2),
        name="gdn",
    )(proj, proj, proj, proj, gates, conv_w, conv_w, conv_w, alog_pad, dtb_pad, out_norm_w.reshape(1, hd))


def _gelu_tanh(x):
    return 0.5 * x * (1.0 + jnp.tanh(math.sqrt(2.0 / math.pi) * (x + 0.044715 * (x * x * x))))


def _lru_kernel(x_ref, y_ref, cw_ref, cb_ref, wa_ref, ba_ref, wx_ref, bx_ref, lam_ref, o_ref):
    xc = _causal_conv(x_ref[...], cw_ref[...]) + cb_ref[...]
    x16 = xc.astype(BF16)
    r = _sigmoid(_dot(x16, wa_ref[0]) + ba_ref[...])
    i = _sigmoid(_dot(x16, wx_ref[0]) + bx_ref[...])
    log_a = (-LRU_C) * r * _softplus(-lam_ref[...])
    a = jnp.exp(log_a)
    th = jnp.tanh(log_a)
    b = jnp.sqrt(-2.0 * th / (1.0 - th)) * i * xc
    row = lax.broadcasted_iota(jnp.int32, a.shape, 0)
    s_len = a.shape[0]
    sh = 1
    while sh < s_len:
        live = row >= sh
        b = a * jnp.where(live, pltpu.roll(b, sh, 0), 0.0) + b
        if sh * 2 < s_len:
            a = a * jnp.where(live, pltpu.roll(a, sh, 0), 1.0)
        sh *= 2
    o_ref[...] = (b * _gelu_tanh(y_ref[...])).astype(o_ref.dtype)


def _lru(proj, conv_w, conv_b, wa, ba, wx, bx, lam, *, batch, col_off):
    t = proj.shape[0]
    s_len = t // batch
    n_blocks, blk = wa.shape[0], wa.shape[1]
    seq = lambda off: pl.BlockSpec((s_len, blk), lambda b, n: (b, off + n))
    vec = pl.BlockSpec((1, blk), lambda b, n: (0, n))
    mat = pl.BlockSpec((1, blk, blk), lambda b, n: (n, 0, 0))
    width = n_blocks * blk
    return pl.pallas_call(
        _lru_kernel,
        grid=(batch, n_blocks),
        in_specs=[seq(col_off), seq(col_off + n_blocks),
                  pl.BlockSpec((CONV_WIDTH, blk), lambda b, n: (0, n)), vec, mat, vec, mat, vec, vec],
        out_specs=pl.BlockSpec((s_len, blk), lambda b, n: (b, n)),
        out_shape=jax.ShapeDtypeStruct((t, width), BF16),
        compiler_params=_cparams(2),
        name="rglru",
    )(proj, proj, conv_w, conv_b.reshape(1, width), wa, ba.reshape(1, width), wx, bx.reshape(1, width),
      lam.reshape(1, width))


def _softmax_pv(scores, v16):
    mx = jnp.max(scores, axis=-1, keepdims=True)
    p = jnp.exp(scores - mx)
    denom = jnp.sum(p, axis=-1, keepdims=True)
    return _dot(p.astype(BF16), v16) / denom


def _fox_kernel(q_ref, k_ref, v_ref, g_ref, fb_ref, o_ref, cfc_s, cfr_s, *, n_heads):
    h = pl.program_id(1)
    s_len = q_ref.shape[0]
    tq = ATTN_Q_TILE

    @pl.when(h == 0)
    def _():
        z = g_ref[...] + fb_ref[...]
        log_f = -_softplus(-z)
        rows = _lane_cumsum(log_f.T[0:n_heads, :], s_len)
        for r in range(n_heads):
            cfr_s[r] = rows[r:r + 1, :]
        cfc_s[...] = _rows_to_columns(rows, LANES)

    cf_col = _lane_column(cfc_s[...], h)
    cf_row = cfr_s[h]
    k16 = k_ref[...].astype(BF16)
    v16 = v_ref[...].astype(BF16)
    scale = HEAD_DIM ** -0.5
    for i in range(s_len // tq):
        hi = (i + 1) * tq
        q16 = q_ref[i * tq:hi, :].astype(BF16)
        sc = _dot_nt(q16, k16[:hi, :]) * scale + cf_col[i * tq:hi, :] - cf_row[:, :hi]
        q_pos = i * tq + lax.broadcasted_iota(jnp.int32, sc.shape, 0)
        k_pos = lax.broadcasted_iota(jnp.int32, sc.shape, 1)
        sc = jnp.where(k_pos <= q_pos, sc, -jnp.inf)
        o_ref[i * tq:hi, :] = _softmax_pv(sc, v16[:hi, :]).astype(o_ref.dtype)


def _fox(proj, gates, fb_pad, *, batch, n_heads):
    t = proj.shape[0]
    s_len = t // batch
    hd = HEAD_DIM
    seq = lambda off: pl.BlockSpec((s_len, hd), lambda b, h: (b, off + h))
    return pl.pallas_call(
        functools.partial(_fox_kernel, n_heads=n_heads),
        grid=(batch, n_heads),
        in_specs=[seq(0), seq(n_heads), seq(2 * n_heads),
                  pl.BlockSpec((s_len, LANES), lambda b, h: (b, 0)),
                  pl.BlockSpec((1, LANES), lambda b, h: (0, 0))],
        out_specs=pl.BlockSpec((s_len, hd), lambda b, h: (b, h)),
        out_shape=jax.ShapeDtypeStruct((t, n_heads * hd), BF16),
        scratch_shapes=[pltpu.VMEM((s_len, LANES), F32), pltpu.VMEM((n_heads, 1, s_len), F32)],
        compiler_params=_cparams(2),
        name="fox",
    )(proj, proj, proj, gates, fb_pad)


def _rotary(x, cos_t, sin_lo, sin_hi):
    half = ROPE_DIM // 2
    return x * cos_t + pltpu.roll(x, LANES - half, 1) * sin_lo + pltpu.roll(x, half, 1) * sin_hi


def _moba_kernel(q_ref, k_ref, v_ref, cos_ref, slo_ref, shi_ref, o_ref, q_s, sel_s):
    s_len = q_ref.shape[0]
    blk = MOBA_BLOCK
    nb = s_len // blk
    cos_t, sin_lo, sin_hi = cos_ref[...], slo_ref[...], shi_ref[...]
    q = _rotary(q_ref[...], cos_t, sin_lo, sin_hi)
    k = _rotary(k_ref[...], cos_t, sin_lo, sin_hi)
    q_s[...] = q.astype(BF16)
    k16 = k.astype(BF16)
    v16 = v_ref[...].astype(BF16)

    k_mean = jnp.concatenate([jnp.mean(k[n * blk:(n + 1) * blk, :], axis=0, keepdims=True) for n in range(nb)]
                             + [jnp.zeros((LANES - nb, HEAD_DIM), F32)], axis=0)
    gate = lax.dot_general(q, k_mean, (((1,), (1,)), ((), ())), preferred_element_type=F32,
                           precision=lax.Precision.HIGHEST)
    lane = lax.broadcasted_iota(jnp.int32, gate.shape, 1)
    q_blk = lax.broadcasted_iota(jnp.int32, gate.shape, 0) // blk
    past = lane < q_blk
    gate = jnp.where(past, gate, -jnp.inf)
    rank = jnp.zeros(gate.shape, F32)
    for m in range(nb - 1):
        g_m = jnp.broadcast_to(gate[:, m:m + 1], gate.shape)
        ahead = (g_m > gate) | ((g_m == gate) & (lane > m))
        rank = rank + jnp.where(ahead & (q_blk > m), 1.0, 0.0)
    sel_s[...] = jnp.where(past & (rank < MOBA_TOPK), 1.0, 0.0)

    scale = HEAD_DIM ** -0.5
    for i in range(nb):
        lo, hi = i * blk, (i + 1) * blk
        sc = _dot_nt(q_s[lo:hi, :], k16[:hi, :]) * scale
        sel = sel_s[lo:hi, :]
        r = lax.broadcasted_iota(jnp.int32, (blk, blk), 0)
        cidx = lax.broadcasted_iota(jnp.int32, (blk, blk), 1)
        parts = [jnp.where(sel[:, j:j + 1] > 0.5, sc[:, j * blk:(j + 1) * blk], -jnp.inf) for j in range(i)]
        parts.append(jnp.where(cidx <= r, sc[:, lo:hi], -jnp.inf))
        sc = parts[0] if len(parts) == 1 else jnp.concatenate(parts, axis=1)
        o_ref[lo:hi, :] = _softmax_pv(sc, v16[:hi, :]).astype(o_ref.dtype)


def _rope_tables(s_len):
    half = ROPE_DIM // 2
    inv_freq = ROPE_THETA ** (-jnp.arange(half, dtype=F32) / half)
    ang = jnp.arange(s_len, dtype=F32)[:, None] * inv_freq[None, :]
    cos, sin = jnp.cos(ang), jnp.sin(ang)
    zeros = jnp.zeros((s_len, HEAD_DIM - ROPE_DIM), F32)
    z_half = jnp.zeros((s_len, half), F32)
    cos_t = jnp.concatenate([cos, cos, zeros + 1.0], axis=1)
    sin_lo = jnp.concatenate([-sin, z_half, zeros], axis=1)
    sin_hi = jnp.concatenate([z_half, sin, zeros], axis=1)
    return cos_t, sin_lo, sin_hi


def _moba(proj, *, batch, n_heads, col_off):
    t = proj.shape[0]
    s_len = t // batch
    hd = HEAD_DIM
    seq = lambda off: pl.BlockSpec((s_len, hd), lambda b, h: (b, off + h))
    tab = pl.BlockSpec((s_len, hd), lambda b, h: (0, 0))
    return pl.pallas_call(
        _moba_kernel,
        grid=(batch, n_heads),
        in_specs=[seq(col_off), seq(col_off + n_heads), seq(col_off + 2 * n_heads), tab, tab, tab],
        out_specs=pl.BlockSpec((s_len, hd), lambda b, h: (b, h)),
        out_shape=jax.ShapeDtypeStruct((t, n_heads * hd), BF16),
        scratch_shapes=[pltpu.VMEM((s_len, hd), BF16), pltpu.VMEM((s_len, LANES), F32)],
        compiler_params=_cparams(2),
        name="moba",
    )(proj, proj, proj, *_rope_tables(s_len))


def _pad_lanes(parts):
    row = jnp.concatenate([p.reshape(1, -1).astype(F32) for p in parts], axis=1)
    return jnp.pad(row, ((0, 0), (0, LANES - row.shape[1])))


def _split_w_in(w, sizes, main_idx, gate_idx):
    offs = [0]
    for n in sizes:
        offs.append(offs[-1] + n)
    cols = lambda i: w[:, offs[i]:offs[i + 1]]
    main = jnp.concatenate([cols(i) for i in main_idx], axis=1).astype(BF16)
    gate = jnp.concatenate([cols(i) for i in gate_idx], axis=1)
    gate = jnp.pad(gate, ((0, 0), (0, LANES - gate.shape[1]))).astype(BF16)
    return main, gate


def kernel(x, ab_norm, ab_w_in, ab_conv_qkv, ab_a_log, ab_dt_bias, ab_out_norm, ab_lru_conv_w, ab_lru_conv_b,
           ab_lru_wa, ab_lru_ba, ab_lru_wx, ab_lru_bx, ab_lru_lambda, ab_w_out, cd_norm, cd_w_in, cd_f_bias,
           cd_w_out, ffn_norm, ffn_w_gate, ffn_w_up, ffn_w_down, final_norm):
    batch, s_len, d_model = x.shape
    depth = ffn_norm.shape[0]
    gdn_heads = ab_a_log.shape[1]
    gdn_width = gdn_heads * HEAD_DIM
    lru_width = ab_lru_lambda.shape[1]
    fox_heads = cd_f_bias.shape[1]
    fox_width = fox_heads * HEAD_DIM
    moba_width = d_model - fox_width
    moba_heads = moba_width // HEAD_DIM
    ab_sizes = (gdn_width,) * 4 + (gdn_heads,) * 2 + (lru_width,) * 2
    cd_sizes = (fox_width,) * 3 + (fox_heads,) + (moba_width,) * 3

    h = x.reshape(batch * s_len, d_model)
    for layer in range(depth):
        j = layer // 2
        if layer % 2 == 0:
            w_main, w_gate = _split_w_in(ab_w_in[j], ab_sizes, (0, 1, 2, 3, 6, 7), (4, 5))
            proj, gates = _inproj(h, ab_norm[j], w_main, w_gate)
            zeros_h = jnp.zeros((gdn_heads,), F32)
            o_a = _gdn(proj, gates, ab_conv_qkv[j], _pad_lanes([zeros_h, ab_a_log[j]]),
                       _pad_lanes([zeros_h, ab_dt_bias[j]]), ab_out_norm[j], batch=batch, n_heads=gdn_heads)
            o_b = _lru(proj, ab_lru_conv_w[j], ab_lru_conv_b[j], ab_lru_wa[j].astype(BF16), ab_lru_ba[j],
                       ab_lru_wx[j].astype(BF16), ab_lru_bx[j], ab_lru_lambda[j], batch=batch,
                       col_off=4 * gdn_heads)
            w_out = ab_w_out[j].astype(BF16)
            h = _outproj(h, o_a, o_b, w_out[:gdn_width], w_out[gdn_width:])
        else:
            w_main, w_gate = _split_w_in(cd_w_in[j], cd_sizes, (0, 1, 2, 4, 5, 6), (3,))
            proj, gates = _inproj(h, cd_norm[j], w_main, w_gate)
            o_c = _fox(proj, gates, _pad_lanes([cd_f_bias[j]]), batch=batch, n_heads=fox_heads)
            o_d = _moba(proj, batch=batch, n_heads=moba_heads, col_off=3 * fox_heads)
            w_out = cd_w_out[j].astype(BF16)
            h = _outproj(h, o_c, o_d, w_out[:fox_width], w_out[fox_width:])
        h = _ffn(h, ffn_norm[layer], ffn_w_gate[layer].astype(BF16), ffn_w_up[layer].astype(BF16),
                 ffn_w_down[layer].astype(BF16), final_norm, final_norm=(layer == depth - 1))
    return h.reshape(batch, s_len, d_model)
```

```python
import functools
import math

import jax
import jax.numpy as jnp
from jax import lax
from jax.experimental import pallas as pl
from jax.experimental.pallas import tpu as pltpu

F32 = jnp.float32
BF16 = jnp.bfloat16

HEAD_DIM = 128
LANES = 128
NORM_EPS = 1e-6
CONV_WIDTH = 4
GDN_CHUNK = 64
GDN_GROUP = 4
GDN_HEADS_PER_STEP = 2
LRU_C = 8.0
MOBA_BLOCK = 256
MOBA_TOPK = 3
ROPE_THETA = 500000.0
ROPE_DIM = HEAD_DIM // 4
ATTN_Q_TILE = 256
VMEM_LIMIT_BYTES = 48 * 1024 * 1024


def _cparams(n_axes):
    return pltpu.CompilerParams(dimension_semantics=("arbitrary",) * n_axes,
                                vmem_limit_bytes=VMEM_LIMIT_BYTES)


def _dot(a, b):
    return jnp.dot(a, b, preferred_element_type=F32)


def _dot_nt(a, b):
    return lax.dot_general(a, b, (((1,), (1,)), ((), ())), preferred_element_type=F32)


def _dot_tn(a, b):
    return lax.dot_general(a, b, (((0,), (0,)), ((), ())), preferred_element_type=F32)


def _rmsnorm(x, w):
    ms = jnp.mean(x * x, axis=-1, keepdims=True)
    return x * lax.rsqrt(ms + NORM_EPS) * w


def _sigmoid(x):
    return 0.5 * jnp.tanh(0.5 * x) + 0.5


def _softplus(x):
    return jnp.maximum(x, 0.0) + jnp.log1p(jnp.exp(-jnp.abs(x)))


def _silu(x):
    return x * _sigmoid(x)


CONV_PAD = 8


def _causal_conv(x, w, pad_ref):
    s_len = x.shape[0]
    k = w.shape[0]
    pad_ref[CONV_PAD:, :] = x
    acc = x * w[k - 1:k, :]
    for j in range(k - 1):
        acc = acc + pad_ref[CONV_PAD - (k - 1 - j):CONV_PAD - (k - 1 - j) + s_len, :] * w[j:j + 1, :]
    return acc


def _lane_column(x, idx):
    lane = lax.broadcasted_iota(jnp.int32, x.shape, 1)
    return jnp.sum(jnp.where(lane == idx, x, 0.0), axis=-1, keepdims=True)


def _lane_cumsum(x, period):
    pos = lax.broadcasted_iota(jnp.int32, x.shape, 1) % period
    sh = 1
    while sh < period:
        x = x + jnp.where(pos >= sh, pltpu.roll(x, sh, 1), 0.0)
        sh *= 2
    return x


def _rows_to_columns(rows8, n_lanes):
    pad = jnp.zeros((n_lanes - rows8.shape[0], rows8.shape[1]), F32)
    return jnp.concatenate([rows8, pad], axis=0).T


def _inproj_kernel(x_ref, nw_ref, w_ref, wg_ref, o_ref, og_ref, xn_ref):
    @pl.when(pl.program_id(1) == 0)
    def _():
        xn = _rmsnorm(x_ref[...], nw_ref[...]).astype(BF16)
        xn_ref[...] = xn
        og_ref[...] = _dot(xn, wg_ref[...])

    o_ref[...] = _dot(xn_ref[...], w_ref[...])


def _inproj(x, nw, w, wg, *, tm=1024, tn=512):
    t, d = x.shape
    n = w.shape[1]
    return pl.pallas_call(
        _inproj_kernel,
        grid=(t // tm, n // tn),
        in_specs=[pl.BlockSpec((tm, d), lambda i, j: (i, 0)),
                  pl.BlockSpec((1, d), lambda i, j: (0, 0)),
                  pl.BlockSpec((d, tn), lambda i, j: (0, j)),
                  pl.BlockSpec((d, LANES), lambda i, j: (0, 0))],
        out_specs=[pl.BlockSpec((tm, tn), lambda i, j: (i, j)),
                   pl.BlockSpec((tm, LANES), lambda i, j: (i, 0))],
        out_shape=[jax.ShapeDtypeStruct((t, n), F32), jax.ShapeDtypeStruct((t, LANES), F32)],
        scratch_shapes=[pltpu.VMEM((tm, d), BF16)],
        compiler_params=_cparams(2),
        name="inproj",
    )(x, nw.reshape(1, d), w, wg)


def _outproj_kernel(res_ref, a1_ref, a2_ref, w1_ref, w2_ref, o_ref):
    o_ref[...] = res_ref[...] + _dot(a1_ref[...], w1_ref[...]) + _dot(a2_ref[...], w2_ref[...])


def _outproj(res, a1, a2, w, *, tm=1024, tn=512):
    t, d = res.shape
    k1, k2 = a1.shape[1], a2.shape[1]
    assert k1 == k2 and w.shape == (k1 + k2, d)
    return pl.pallas_call(
        _outproj_kernel,
        grid=(t // tm, d // tn),
        in_specs=[pl.BlockSpec((tm, tn), lambda i, j: (i, j)),
                  pl.BlockSpec((tm, k1), lambda i, j: (i, 0)),
                  pl.BlockSpec((tm, k2), lambda i, j: (i, 0)),
                  pl.BlockSpec((k1, tn), lambda i, j: (0, j)),
                  pl.BlockSpec((k2, tn), lambda i, j: (1, j))],
        out_specs=pl.BlockSpec((tm, tn), lambda i, j: (i, j)),
        out_shape=jax.ShapeDtypeStruct((t, d), F32),
        compiler_params=_cparams(2),
        name="outproj",
    )(res, a1, a2, w, w)


def _ffn_kernel(x_ref, nw_ref, wg_ref, wu_ref, wd_ref, fnw_ref, o_ref, xn_ref, acc_ref, *, final_norm):
    f = pl.program_id(1)

    @pl.when(f == 0)
    def _():
        xn_ref[...] = _rmsnorm(x_ref[...], nw_ref[...]).astype(BF16)
        acc_ref[...] = jnp.zeros_like(acc_ref)

    xn = xn_ref[...]
    gate = _dot(xn, wg_ref[...])
    up = _dot(xn, wu_ref[...])
    acc_ref[...] += _dot((_silu(gate) * up).astype(BF16), wd_ref[...])

    @pl.when(f == pl.num_programs(1) - 1)
    def _():
        h = x_ref[...] + acc_ref[...]
        if final_norm:
            h = _rmsnorm(h, fnw_ref[...])
        o_ref[...] = h


def _ffn(x, nw, wg, wu, wd, fnw, *, final_norm, tm=512, tf=512):
    t, d = x.shape
    dff = wg.shape[1]
    return pl.pallas_call(
        functools.partial(_ffn_kernel, final_norm=final_norm),
        grid=(t // tm, dff // tf),
        in_specs=[pl.BlockSpec((tm, d), lambda i, f: (i, 0)),
                  pl.BlockSpec((1, d), lambda i, f: (0, 0)),
                  pl.BlockSpec((d, tf), lambda i, f: (0, f)),
                  pl.BlockSpec((d, tf), lambda i, f: (0, f)),
                  pl.BlockSpec((tf, d), lambda i, f: (f, 0)),
                  pl.BlockSpec((1, d), lambda i, f: (0, 0))],
        out_specs=pl.BlockSpec((tm, d), lambda i, f: (i, 0)),
        out_shape=jax.ShapeDtypeStruct((t, d), F32),
        scratch_shapes=[pltpu.VMEM((tm, d), BF16), pltpu.VMEM((tm, d), F32)],
        compiler_params=_cparams(2),
        name="ffn",
    )(x, nw.reshape(1, d), wg, wu, wd, fnw.reshape(1, d))


def _unit_lower_inverses(ms):
    c = ms[0].shape[0]
    eye = (lax.broadcasted_iota(jnp.int32, (c, c), 0) == lax.broadcasted_iota(jnp.int32, (c, c), 1)).astype(F32)
    invs = [eye - m for m in ms]
    m16s = [m.astype(BF16) for m in ms]
    ps = [_dot(m16, m16) for m16 in m16s]
    span = 2
    while span < c:
        p16s = [p.astype(BF16) for p in ps]
        if span * 2 < c:
            boths = [_dot(jnp.concatenate([inv.astype(BF16), p16], axis=0), p16) for inv, p16 in zip(invs, p16s)]
            invs = [inv + both[:c] for inv, both in zip(invs, boths)]
            ps = [both[c:] for both in boths]
        else:
            invs = [inv + _dot(inv.astype(BF16), p16) for inv, p16 in zip(invs, p16s)]
        span *= 2
    return invs


def _gdn_kernel(q_ref, k_ref, v_ref, z_ref, g_ref, cq_ref, ck_ref, cv_ref, alog_ref, dtb_ref, onw_ref,
                o_ref, beta_s, gcc_s, gcr_s, q_s, k_s, v_s, bc_s, gc_s, p_s, b_s, qp_s, out_s, pad_s,
                *, n_heads, hp):
    hb = pl.program_id(1)
    s_len = q_ref.shape[0]
    c = GDN_CHUNK
    group = GDN_GROUP
    hd = HEAD_DIM

    @pl.when(hb == 0)
    def _():
        logits = g_ref[...]
        beta_s[...] = _sigmoid(logits)
        log_decay = -jnp.exp(alog_ref[...]) * _softplus(logits + dtb_ref[...])
        rows = log_decay.T[n_heads:2 * n_heads, :]
        rows = _lane_cumsum(rows, c)
        for r in range(n_heads):
            gcr_s[r] = rows[r:r + 1, :]
        gcc_s[...] = _rows_to_columns(rows, LANES)

    pad_s[:CONV_PAD, :] = jnp.zeros((CONV_PAD, hd), F32)
    for j in range(hp):
        cols = slice(j * hd, (j + 1) * hd)
        head = hb * hp + j
        bc_s[j] = jnp.broadcast_to(_lane_column(beta_s[...], head), (s_len, hd))
        gc_s[j] = jnp.broadcast_to(_lane_column(gcc_s[...], head), (s_len, hd))
        q = _silu(_causal_conv(q_ref[:, cols], cq_ref[:, cols], pad_s))
        k = _silu(_causal_conv(k_ref[:, cols], ck_ref[:, cols], pad_s))
        v_s[j] = _silu(_causal_conv(v_ref[:, cols], cv_ref[:, cols], pad_s))
        q_s[j] = q * lax.rsqrt(jnp.sum(q * q, axis=-1, keepdims=True) + NORM_EPS) * (hd ** -0.5)
        k_s[j] = k * lax.rsqrt(jnp.sum(k * k, axis=-1, keepdims=True) + NORM_EPS)

    ri = lax.broadcasted_iota(jnp.int32, (c, c), 0)
    ci = lax.broadcasted_iota(jnp.int32, (c, c), 1)
    incl = ri >= ci
    strict = ri > ci

    def group_body(gi, carry):
        lanes = pl.ds(pl.multiple_of(gi * (group * c), group * c), group * c)
        gc_rows = [gcr_s[hb * hp + j, :, lanes] for j in range(hp)]
        units = [(j, g) for g in range(group) for j in range(hp)]
        rows_u, scores_u, decay_u, rhs_u, kdec_u, qdec_u = [], [], [], [], [], []
        for j, g in units:
            rows = pl.ds(pl.multiple_of((gi * group + g) * c, c), c)
            qc, kc, vc = q_s[j, rows, :], k_s[j, rows, :], v_s[j, rows, :]
            beta, gc = bc_s[j, rows, :], gc_s[j, rows, :]
            gc_row = gc_rows[j][:, g * c:(g + 1) * c]
            decay_u.append(jnp.where(incl, jnp.exp(jnp.where(incl, gc[:, :c] - gc_row, 0.0)), 0.0))
            k_beta = kc * beta
            e_gc = jnp.exp(gc)
            lhs = jnp.concatenate([k_beta.astype(BF16), qc.astype(BF16)], axis=0)
            scores_u.append(_dot_nt(lhs, kc.astype(BF16)))
            rhs_u.append(jnp.concatenate([(vc * beta).astype(BF16), (k_beta * e_gc).astype(BF16)], axis=1))
            kdec_u.append((kc * jnp.exp(gc[c - 1:c, :] - gc)).astype(BF16))
            qdec_u.append(qc * e_gc)
            rows_u.append(rows)
        ms = [jnp.where(strict, sc[:c] * dec, 0.0) for sc, dec in zip(scores_u, decay_u)]
        attn_u = [jnp.where(incl, sc[c:] * dec, 0.0).astype(BF16) for sc, dec in zip(scores_u, decay_u)]
        t_u = _unit_lower_inverses(ms)
        uw_u = [_dot(t.astype(BF16), rhs).astype(BF16) for t, rhs in zip(t_u, rhs_u)]
        bp_u = [_dot_tn(kdec, uw) for kdec, uw in zip(kdec_u, uw_u)]
        aw_u = [_dot(attn, uw) for attn, uw in zip(attn_u, uw_u)]
        for (j, g), rows, bp, aw, qdec in zip(units, rows_u, bp_u, aw_u, qdec_u):
            n = gi * group + g
            b_s[j, n] = bp[:, :hd]
            p_s[j, n] = bp[:, hd:].astype(BF16)
            qp_s[j, rows, :] = (qdec - aw[:, hd:]).astype(BF16)
            out_s[j, rows, :] = aw[:, :hd]
        return carry

    lax.fori_loop(0, s_len // (group * c), group_body, 0)

    def seq_body(n, states):
        r0 = pl.multiple_of(n * c, c)
        rows = pl.ds(r0, c)
        nxt = []
        for j in range(hp):
            s16 = states[j].astype(BF16)
            out_s[j, rows, :] += _dot(qp_s[j, rows, :], s16)
            gc_tail = gc_s[j, pl.ds(pl.multiple_of(r0 + c - 8, 8), 8), :]
            nxt.append(states[j] * jnp.exp(gc_tail[7:8, :]) - _dot(p_s[j, n], s16) + b_s[j, n])
        return tuple(nxt)

    lax.fori_loop(0, s_len // c, seq_body, tuple(jnp.zeros((hd, hd), F32) for _ in range(hp)), unroll=2)

    for j in range(hp):
        cols = slice(j * hd, (j + 1) * hd)
        o = _rmsnorm(out_s[j], onw_ref[...]) * _silu(z_ref[:, cols])
        o_ref[:, cols] = o.astype(o_ref.dtype)


def _gdn(proj, gates, conv_w, alog_pad, dtb_pad, out_norm_w, *, batch, n_heads, hp=GDN_HEADS_PER_STEP):
    t = proj.shape[0]
    s_len = t // batch
    hd = HEAD_DIM
    nblk = n_heads // hp
    n_chunks = s_len // GDN_CHUNK
    seq = lambda off: pl.BlockSpec((s_len, hp * hd), lambda b, h: (b, off + h))
    cw = lambda off: pl.BlockSpec((CONV_WIDTH, hp * hd), lambda b, h: (0, off + h))
    row = pl.BlockSpec((1, LANES), lambda b, h: (0, 0))
    act = pltpu.VMEM((s_len, LANES), F32)
    per_head = pltpu.VMEM((hp, s_len, hd), F32)
    return pl.pallas_call(
        functools.partial(_gdn_kernel, n_heads=n_heads, hp=hp),
        grid=(batch, nblk),
        in_specs=[seq(0), seq(nblk), seq(2 * nblk), seq(3 * nblk),
                  pl.BlockSpec((s_len, LANES), lambda b, h: (b, 0)),
                  cw(0), cw(nblk), cw(2 * nblk), row, row, row],
        out_specs=pl.BlockSpec((s_len, hp * hd), lambda b, h: (b, h)),
        out_shape=jax.ShapeDtypeStruct((t, n_heads * hd), BF16),
        scratch_shapes=[act, act, pltpu.VMEM((n_heads, 1, s_len), F32),
                        per_head, per_head, per_head, per_head, per_head,
                        pltpu.VMEM((hp, n_chunks, hd, hd), BF16), pltpu.VMEM((hp, n_chunks, hd, hd), F32),
                        pltpu.VMEM((hp, s_len, hd), BF16), per_head, pltpu.VMEM((CONV_PAD + s_len, hd), F32)],
        compiler_params=_cparams(2),
        name="gdn",
    )(proj, proj, proj, proj, gates, conv_w, conv_w, conv_w, alog_pad, dtb_pad, out_norm_w.reshape(1, hd))


def _gelu_tanh(x):
    return 0.5 * x * (1.0 + jnp.tanh(math.sqrt(2.0 / math.pi) * (x + 0.044715 * (x * x * x))))


def _lru_kernel(x_ref, y_ref, cw_ref, cb_ref, wa_ref, ba_ref, wx_ref, bx_ref, lam_ref, o_ref, pad_s):
    pad_s[:CONV_PAD, :] = jnp.zeros((CONV_PAD, pad_s.shape[1]), F32)
    xc = _causal_conv(x_ref[...], cw_ref[...], pad_s) + cb_ref[...]
    x16 = xc.astype(BF16)
    r = _sigmoid(_dot(x16, wa_ref[0]) + ba_ref[...])
    i = _sigmoid(_dot(x16, wx_ref[0]) + bx_ref[...])
    log_a = (-LRU_C) * r * _softplus(-lam_ref[...])
    a = jnp.exp(log_a)
    th = jnp.tanh(log_a)
    b = jnp.sqrt(-2.0 * th / (1.0 - th)) * i * xc
    row = lax.broadcasted_iota(jnp.int32, a.shape, 0)
    s_len = a.shape[0]
    sh = 1
    while sh < s_len:
        live = row >= sh
        b = a * jnp.where(live, pltpu.roll(b, sh, 0), 0.0) + b
        if sh * 2 < s_len:
            a = a * jnp.where(live, pltpu.roll(a, sh, 0), 1.0)
        sh *= 2
    o_ref[...] = (b * _gelu_tanh(y_ref[...])).astype(o_ref.dtype)


def _lru(proj, conv_w, conv_b, wa, ba, wx, bx, lam, *, batch, col_off):
    t = proj.shape[0]
    s_len = t // batch
    n_blocks, blk = wa.shape[0], wa.shape[1]
    seq = lambda off: pl.BlockSpec((s_len, blk), lambda b, n: (b, off + n))
    vec = pl.BlockSpec((1, blk), lambda b, n: (0, n))
    mat = pl.BlockSpec((1, blk, blk), lambda b, n: (n, 0, 0))
    width = n_blocks * blk
    return pl.pallas_call(
        _lru_kernel,
        grid=(batch, n_blocks),
        in_specs=[seq(col_off), seq(col_off + n_blocks),
                  pl.BlockSpec((CONV_WIDTH, blk), lambda b, n: (0, n)), vec, mat, vec, mat, vec, vec],
        out_specs=pl.BlockSpec((s_len, blk), lambda b, n: (b, n)),
        out_shape=jax.ShapeDtypeStruct((t, width), BF16),
        scratch_shapes=[pltpu.VMEM((CONV_PAD + s_len, blk), F32)],
        compiler_params=_cparams(2),
        name="rglru",
    )(proj, proj, conv_w, conv_b.reshape(1, width), wa, ba.reshape(1, width), wx, bx.reshape(1, width),
      lam.reshape(1, width))


def _softmax_pv(scores, v16):
    mx = jnp.max(scores, axis=-1, keepdims=True)
    p = jnp.exp(scores - mx)
    denom = jnp.sum(p, axis=-1, keepdims=True)
    return _dot(p.astype(BF16), v16) / denom


def _fox_kernel(q_ref, k_ref, v_ref, g_ref, fb_ref, o_ref, cfc_s, cfr_s, *, n_heads):
    h = pl.program_id(1)
    s_len = q_ref.shape[0]
    tq = ATTN_Q_TILE

    @pl.when(h == 0)
    def _():
        z = g_ref[...] + fb_ref[...]
        log_f = -_softplus(-z)
        rows = _lane_cumsum(log_f.T[0:n_heads, :], s_len)
        for r in range(n_heads):
            cfr_s[r] = rows[r:r + 1, :]
        cfc_s[...] = _rows_to_columns(rows, LANES)

    cf_col = _lane_column(cfc_s[...], h)
    cf_row = cfr_s[h]
    k16 = k_ref[...].astype(BF16)
    v16 = v_ref[...].astype(BF16)
    scale = HEAD_DIM ** -0.5
    n_tiles = s_len // tq
    qk = lambda i: _dot_nt(q_ref[i * tq:(i + 1) * tq, :].astype(BF16), k16[:(i + 1) * tq, :])
    qk_next = qk(0)
    for i in range(n_tiles):
        hi = (i + 1) * tq
        qk_cur, qk_next = qk_next, (qk(i + 1) if i + 1 < n_tiles else None)
        sc = qk_cur * scale + cf_col[i * tq:hi, :] - cf_row[:, :hi]
        q_pos = i * tq + lax.broadcasted_iota(jnp.int32, sc.shape, 0)
        k_pos = lax.broadcasted_iota(jnp.int32, sc.shape, 1)
        sc = jnp.where(k_pos <= q_pos, sc, -jnp.inf)
        o_ref[i * tq:hi, :] = _softmax_pv(sc, v16[:hi, :]).astype(o_ref.dtype)


def _fox(proj, gates, fb_pad, *, batch, n_heads):
    t = proj.shape[0]
    s_len = t // batch
    hd = HEAD_DIM
    seq = lambda off: pl.BlockSpec((s_len, hd), lambda b, h: (b, off + h))
    return pl.pallas_call(
        functools.partial(_fox_kernel, n_heads=n_heads),
        grid=(batch, n_heads),
        in_specs=[seq(0), seq(n_heads), seq(2 * n_heads),
                  pl.BlockSpec((s_len, LANES), lambda b, h: (b, 0)),
                  pl.BlockSpec((1, LANES), lambda b, h: (0, 0))],
        out_specs=pl.BlockSpec((s_len, hd), lambda b, h: (b, h)),
        out_shape=jax.ShapeDtypeStruct((t, n_heads * hd), BF16),
        scratch_shapes=[pltpu.VMEM((s_len, LANES), F32), pltpu.VMEM((n_heads, 1, s_len), F32)],
        compiler_params=_cparams(2),
        name="fox",
    )(proj, proj, proj, gates, fb_pad)


def _rotary(x, cos_t, sin_lo, sin_hi):
    half = ROPE_DIM // 2
    return x * cos_t + pltpu.roll(x, LANES - half, 1) * sin_lo + pltpu.roll(x, half, 1) * sin_hi


def _moba_kernel(q_ref, k_ref, v_ref, cos_ref, slo_ref, shi_ref, o_ref, q_s, sel_s):
    s_len = q_ref.shape[0]
    blk = MOBA_BLOCK
    nb = s_len // blk
    cos_t, sin_lo, sin_hi = cos_ref[...], slo_ref[...], shi_ref[...]
    q = _rotary(q_ref[...], cos_t, sin_lo, sin_hi)
    k = _rotary(k_ref[...], cos_t, sin_lo, sin_hi)
    q_s[...] = q.astype(BF16)
    k16 = k.astype(BF16)
    v16 = v_ref[...].astype(BF16)

    k_mean = jnp.concatenate([jnp.mean(k[n * blk:(n + 1) * blk, :], axis=0, keepdims=True) for n in range(nb)]
                             + [jnp.zeros((LANES - nb, HEAD_DIM), F32)], axis=0)
    gate = lax.dot_general(q, k_mean, (((1,), (1,)), ((), ())), preferred_element_type=F32,
                           precision=lax.Precision.HIGHEST)
    lane = lax.broadcasted_iota(jnp.int32, gate.shape, 1)
    q_blk = lax.broadcasted_iota(jnp.int32, gate.shape, 0) // blk
    past = lane < q_blk
    gate = jnp.where(past, gate, -jnp.inf)
    rank = jnp.zeros(gate.shape, F32)
    for m in range(nb - 1):
        g_m = jnp.broadcast_to(gate[:, m:m + 1], gate.shape)
        ahead = (g_m > gate) | ((g_m == gate) & (lane > m))
        rank = rank + jnp.where(ahead & (q_blk > m), 1.0, 0.0)
    sel_s[...] = jnp.where(past & (rank < MOBA_TOPK), 1.0, 0.0)

    scale = HEAD_DIM ** -0.5
    for i in range(nb):
        lo, hi = i * blk, (i + 1) * blk
        sc = _dot_nt(q_s[lo:hi, :], k16[:hi, :]) * scale
        sel = sel_s[lo:hi, :]
        r = lax.broadcasted_iota(jnp.int32, (blk, blk), 0)
        cidx = lax.broadcasted_iota(jnp.int32, (blk, blk), 1)
        parts = [jnp.where(sel[:, j:j + 1] > 0.5, sc[:, j * blk:(j + 1) * blk], -jnp.inf) for j in range(i)]
        parts.append(jnp.where(cidx <= r, sc[:, lo:hi], -jnp.inf))
        sc = parts[0] if len(parts) == 1 else jnp.concatenate(parts, axis=1)
        o_ref[lo:hi, :] = _softmax_pv(sc, v16[:hi, :]).astype(o_ref.dtype)


def _rope_tables(s_len):
    half = ROPE_DIM // 2
    inv_freq = ROPE_THETA ** (-jnp.arange(half, dtype=F32) / half)
    ang = jnp.arange(s_len, dtype=F32)[:, None] * inv_freq[None, :]
    cos, sin = jnp.cos(ang), jnp.sin(ang)
    zeros = jnp.zeros((s_len, HEAD_DIM - ROPE_DIM), F32)
    z_half = jnp.zeros((s_len, half), F32)
    cos_t = jnp.concatenate([cos, cos, zeros + 1.0], axis=1)
    sin_lo = jnp.concatenate([-sin, z_half, zeros], axis=1)
    sin_hi = jnp.concatenate([z_half, sin, zeros], axis=1)
    return cos_t, sin_lo, sin_hi


def _moba(proj, *, batch, n_heads, col_off):
    t = proj.shape[0]
    s_len = t // batch
    hd = HEAD_DIM
    seq = lambda off: pl.BlockSpec((s_len, hd), lambda b, h: (b, off + h))
    tab = pl.BlockSpec((s_len, hd), lambda b, h: (0, 0))
    return pl.pallas_call(
        _moba_kernel,
        grid=(batch, n_heads),
        in_specs=[seq(col_off), seq(col_off + n_heads), seq(col_off + 2 * n_heads), tab, tab, tab],
        out_specs=pl.BlockSpec((s_len, hd), lambda b, h: (b, h)),
        out_shape=jax.ShapeDtypeStruct((t, n_heads * hd), BF16),
        scratch_shapes=[pltpu.VMEM((s_len, hd), BF16), pltpu.VMEM((s_len, LANES), F32)],
        compiler_params=_cparams(2),
        name="moba",
    )(proj, proj, proj, *_rope_tables(s_len))


def _pad_lanes(parts):
    row = jnp.concatenate([p.reshape(1, -1).astype(F32) for p in parts], axis=1)
    return jnp.pad(row, ((0, 0), (0, LANES - row.shape[1])))


def _split_w_in(w, sizes, main_idx, gate_idx):
    offs = [0]
    for n in sizes:
        offs.append(offs[-1] + n)
    cols = lambda i: w[:, offs[i]:offs[i + 1]]
    main = jnp.concatenate([cols(i) for i in main_idx], axis=1).astype(BF16)
    gate = jnp.concatenate([cols(i) for i in gate_idx], axis=1)
    gate = jnp.pad(gate, ((0, 0), (0, LANES - gate.shape[1]))).astype(BF16)
    return main, gate


def kernel(x, ab_norm, ab_w_in, ab_conv_qkv, ab_a_log, ab_dt_bias, ab_out_norm, ab_lru_conv_w, ab_lru_conv_b,
           ab_lru_wa, ab_lru_ba, ab_lru_wx, ab_lru_bx, ab_lru_lambda, ab_w_out, cd_norm, cd_w_in, cd_f_bias,
           cd_w_out, ffn_norm, ffn_w_gate, ffn_w_up, ffn_w_down, final_norm):
    batch, s_len, d_model = x.shape
    depth = ffn_norm.shape[0]
    gdn_heads = ab_a_log.shape[1]
    gdn_width = gdn_heads * HEAD_DIM
    lru_width = ab_lru_lambda.shape[1]
    fox_heads = cd_f_bias.shape[1]
    fox_width = fox_heads * HEAD_DIM
    moba_width = d_model - fox_width
    moba_heads = moba_width // HEAD_DIM
    ab_sizes = (gdn_width,) * 4 + (gdn_heads,) * 2 + (lru_width,) * 2
    cd_sizes = (fox_width,) * 3 + (fox_heads,) + (moba_width,) * 3

    h = x.reshape(batch * s_len, d_model)
    for layer in range(depth):
        j = layer // 2
        if layer % 2 == 0:
            w_main, w_gate = _split_w_in(ab_w_in[j], ab_sizes, (0, 1, 2, 3, 6, 7), (4, 5))
            proj, gates = _inproj(h, ab_norm[j], w_main, w_gate)
            zeros_h = jnp.zeros((gdn_heads,), F32)
            o_a = _gdn(proj, gates, ab_conv_qkv[j], _pad_lanes([zeros_h, ab_a_log[j]]),
                       _pad_lanes([zeros_h, ab_dt_bias[j]]), ab_out_norm[j], batch=batch, n_heads=gdn_heads)
            o_b = _lru(proj, ab_lru_conv_w[j], ab_lru_conv_b[j], ab_lru_wa[j].astype(BF16), ab_lru_ba[j],
                       ab_lru_wx[j].astype(BF16), ab_lru_bx[j], ab_lru_lambda[j], batch=batch,
                       col_off=4 * gdn_heads)
            h = _outproj(h, o_a, o_b, ab_w_out[j].astype(BF16))
        else:
            w_main, w_gate = _split_w_in(cd_w_in[j], cd_sizes, (0, 1, 2, 4, 5, 6), (3,))
            proj, gates = _inproj(h, cd_norm[j], w_main, w_gate)
            o_c = _fox(proj, gates, _pad_lanes([cd_f_bias[j]]), batch=batch, n_heads=fox_heads)
            o_d = _moba(proj, batch=batch, n_heads=moba_heads, col_off=3 * fox_heads)
            h = _outproj(h, o_c, o_d, cd_w_out[j].astype(BF16))
        h = _ffn(h, ffn_norm[layer], ffn_w_gate[layer].astype(BF16), ffn_w_up[layer].astype(BF16),
                 ffn_w_down[layer].astype(BF16), final_norm, final_norm=(layer == depth - 1))
    return h.reshape(batch, s_len, d_model)
```

```python
import functools
import math

import jax
import jax.numpy as jnp
from jax import lax
from jax.experimental import pallas as pl
from jax.experimental.pallas import tpu as pltpu

F32 = jnp.float32
BF16 = jnp.bfloat16

HEAD_DIM = 128
LANES = 128
NORM_EPS = 1e-6
CONV_WIDTH = 4
GDN_CHUNK = 64
GDN_GROUP = 4
GDN_HEADS_PER_STEP = 2
LRU_C = 8.0
MOBA_BLOCK = 256
MOBA_TOPK = 3
ROPE_THETA = 500000.0
ROPE_DIM = HEAD_DIM // 4
ATTN_Q_TILE = 256
VMEM_LIMIT_BYTES = 48 * 1024 * 1024
FFN_VMEM_LIMIT_BYTES = 58 * 1024 * 1024


def _cparams(n_axes, vmem_limit_bytes=VMEM_LIMIT_BYTES):
    return pltpu.CompilerParams(dimension_semantics=("arbitrary",) * n_axes,
                                vmem_limit_bytes=vmem_limit_bytes)


def _dot(a, b):
    return jnp.dot(a, b, preferred_element_type=F32)


def _dot_nt(a, b):
    return lax.dot_general(a, b, (((1,), (1,)), ((), ())), preferred_element_type=F32)


def _dot_tn(a, b):
    return lax.dot_general(a, b, (((0,), (0,)), ((), ())), preferred_element_type=F32)


def _rmsnorm(x, w):
    ms = jnp.mean(x * x, axis=-1, keepdims=True)
    return x * lax.rsqrt(ms + NORM_EPS) * w


def _sigmoid(x):
    return 0.5 * jnp.tanh(0.5 * x) + 0.5


def _softplus(x):
    return jnp.maximum(x, 0.0) + jnp.log1p(jnp.exp(-jnp.abs(x)))


def _silu(x):
    return x * _sigmoid(x)


CONV_PAD = 8


def _causal_conv(x, w, pad_ref):
    s_len = x.shape[0]
    k = w.shape[0]
    pad_ref[CONV_PAD:, :] = x
    acc = x * w[k - 1:k, :]
    for j in range(k - 1):
        acc = acc + pad_ref[CONV_PAD - (k - 1 - j):CONV_PAD - (k - 1 - j) + s_len, :] * w[j:j + 1, :]
    return acc


def _lane_column(x, idx):
    lane = lax.broadcasted_iota(jnp.int32, x.shape, 1)
    return jnp.sum(jnp.where(lane == idx, x, 0.0), axis=-1, keepdims=True)


def _lane_cumsum(x, period):
    pos = lax.broadcasted_iota(jnp.int32, x.shape, 1) % period
    sh = 1
    while sh < period:
        x = x + jnp.where(pos >= sh, pltpu.roll(x, sh, 1), 0.0)
        sh *= 2
    return x


def _rows_to_columns(rows8, n_lanes):
    pad = jnp.zeros((n_lanes - rows8.shape[0], rows8.shape[1]), F32)
    return jnp.concatenate([rows8, pad], axis=0).T


def _inproj_kernel(x_ref, nw_ref, w_ref, wg_ref, o_ref, og_ref, xn_ref):
    @pl.when(pl.program_id(1) == 0)
    def _():
        xn = _rmsnorm(x_ref[...], nw_ref[...]).astype(BF16)
        xn_ref[...] = xn
        og_ref[...] = _dot(xn, wg_ref[...])

    o_ref[...] = _dot(xn_ref[...], w_ref[...]).astype(o_ref.dtype)


def _inproj(x, nw, w, wg, *, tm=1024, tn=1024):
    t, d = x.shape
    n = w.shape[1]
    return pl.pallas_call(
        _inproj_kernel,
        grid=(t // tm, n // tn),
        in_specs=[pl.BlockSpec((tm, d), lambda i, j: (i, 0)),
                  pl.BlockSpec((1, d), lambda i, j: (0, 0)),
                  pl.BlockSpec((d, tn), lambda i, j: (0, j)),
                  pl.BlockSpec((d, LANES), lambda i, j: (0, 0))],
        out_specs=[pl.BlockSpec((tm, tn), lambda i, j: (i, j)),
                   pl.BlockSpec((tm, LANES), lambda i, j: (i, 0))],
        out_shape=[jax.ShapeDtypeStruct((t, n), BF16), jax.ShapeDtypeStruct((t, LANES), F32)],
        scratch_shapes=[pltpu.VMEM((tm, d), BF16)],
        compiler_params=_cparams(2),
        name="inproj",
    )(x, nw.reshape(1, d), w, wg)


def _outproj_kernel(res_ref, a1_ref, a2_ref, w1_ref, w2_ref, o_ref):
    o_ref[...] = res_ref[...] + _dot(a1_ref[...], w1_ref[...]) + _dot(a2_ref[...], w2_ref[...])


def _outproj(res, a1, a2, w, *, tm=1024, tn=512):
    t, d = res.shape
    k1, k2 = a1.shape[1], a2.shape[1]
    assert k1 == k2 and w.shape == (k1 + k2, d)
    return pl.pallas_call(
        _outproj_kernel,
        grid=(t // tm, d // tn),
        in_specs=[pl.BlockSpec((tm, tn), lambda i, j: (i, j)),
                  pl.BlockSpec((tm, k1), lambda i, j: (i, 0)),
                  pl.BlockSpec((tm, k2), lambda i, j: (i, 0)),
                  pl.BlockSpec((k1, tn), lambda i, j: (0, j)),
                  pl.BlockSpec((k2, tn), lambda i, j: (1, j))],
        out_specs=pl.BlockSpec((tm, tn), lambda i, j: (i, j)),
        out_shape=jax.ShapeDtypeStruct((t, d), F32),
        compiler_params=_cparams(2),
        name="outproj",
    )(res, a1, a2, w, w)


def _ffn_kernel(x_ref, nw_ref, wg_ref, wu_ref, wd_ref, fnw_ref, o_ref, xn_ref, *, final_norm, down_chunk):
    f = pl.program_id(1)

    @pl.when(f == 0)
    def _():
        x = x_ref[...]
        xn_ref[...] = _rmsnorm(x, nw_ref[...]).astype(BF16)
        o_ref[...] = x

    xn = xn_ref[...]
    act = (_silu(_dot(xn, wg_ref[...])) * _dot(xn, wu_ref[...])).astype(BF16)
    for c0 in range(0, o_ref.shape[1], down_chunk):
        o_ref[:, c0:c0 + down_chunk] += _dot(act, wd_ref[:, c0:c0 + down_chunk])

    if final_norm:
        @pl.when(f == pl.num_programs(1) - 1)
        def _():
            o_ref[...] = _rmsnorm(o_ref[...], fnw_ref[...])


def _ffn(x, nw, wg, wu, wd, fnw, *, final_norm, tm=1024, tf=512, down_chunk=512):
    t, d = x.shape
    dff = wg.shape[1]
    return pl.pallas_call(
        functools.partial(_ffn_kernel, final_norm=final_norm, down_chunk=down_chunk),
        grid=(t // tm, dff // tf),
        in_specs=[pl.BlockSpec((tm, d), lambda i, f: (i, 0)),
                  pl.BlockSpec((1, d), lambda i, f: (0, 0)),
                  pl.BlockSpec((d, tf), lambda i, f: (0, f)),
                  pl.BlockSpec((d, tf), lambda i, f: (0, f)),
                  pl.BlockSpec((tf, d), lambda i, f: (f, 0)),
                  pl.BlockSpec((1, d), lambda i, f: (0, 0))],
        out_specs=pl.BlockSpec((tm, d), lambda i, f: (i, 0)),
        out_shape=jax.ShapeDtypeStruct((t, d), F32),
        scratch_shapes=[pltpu.VMEM((tm, d), BF16)],
        compiler_params=_cparams(2, FFN_VMEM_LIMIT_BYTES),
        name="ffn",
    )(x, nw.reshape(1, d), wg, wu, wd, fnw.reshape(1, d))


def _unit_lower_inverses(ms):
    c = ms[0].shape[0]
    eye = (lax.broadcasted_iota(jnp.int32, (c, c), 0) == lax.broadcasted_iota(jnp.int32, (c, c), 1)).astype(F32)
    invs = [eye - m for m in ms]
    m16s = [m.astype(BF16) for m in ms]
    ps = [_dot(m16, m16) for m16 in m16s]
    span = 2
    while span < c:
        p16s = [p.astype(BF16) for p in ps]
        if span * 2 < c:
            boths = [_dot(jnp.concatenate([inv.astype(BF16), p16], axis=0), p16) for inv, p16 in zip(invs, p16s)]
            invs = [inv + both[:c] for inv, both in zip(invs, boths)]
            ps = [both[c:] for both in boths]
        else:
            invs = [inv + _dot(inv.astype(BF16), p16) for inv, p16 in zip(invs, p16s)]
        span *= 2
    return invs


def _gdn_kernel(q_ref, k_ref, v_ref, z_ref, g_ref, cq_ref, ck_ref, cv_ref, alog_ref, dtb_ref, onw_ref,
                o_ref, beta_s, gcc_s, gcr_s, q_s, k_s, v_s, bc_s, gc_s, p_s, b_s, qp_s, out_s, pad_s,
                *, n_heads, hp):
    hb = pl.program_id(1)
    s_len = q_ref.shape[0]
    c = GDN_CHUNK
    group = GDN_GROUP
    hd = HEAD_DIM

    @pl.when(hb == 0)
    def _():
        logits = g_ref[...]
        beta_s[...] = _sigmoid(logits)
        log_decay = -jnp.exp(alog_ref[...]) * _softplus(logits + dtb_ref[...])
        rows = log_decay.T[n_heads:2 * n_heads, :]
        rows = _lane_cumsum(rows, c)
        for r in range(n_heads):
            gcr_s[r] = rows[r:r + 1, :]
        gcc_s[...] = _rows_to_columns(rows, LANES)

    pad_s[:CONV_PAD, :] = jnp.zeros((CONV_PAD, hd), F32)
    for j in range(hp):
        cols = slice(j * hd, (j + 1) * hd)
        head = hb * hp + j
        bc_s[j] = jnp.broadcast_to(_lane_column(beta_s[...], head), (s_len, hd))
        gc_s[j] = jnp.broadcast_to(_lane_column(gcc_s[...], head), (s_len, hd))
        q = _silu(_causal_conv(q_ref[:, cols].astype(F32), cq_ref[:, cols], pad_s))
        k = _silu(_causal_conv(k_ref[:, cols].astype(F32), ck_ref[:, cols], pad_s))
        v_s[j] = _silu(_causal_conv(v_ref[:, cols].astype(F32), cv_ref[:, cols], pad_s))
        q_s[j] = q * lax.rsqrt(jnp.sum(q * q, axis=-1, keepdims=True) + NORM_EPS) * (hd ** -0.5)
        k_s[j] = k * lax.rsqrt(jnp.sum(k * k, axis=-1, keepdims=True) + NORM_EPS)

    ri = lax.broadcasted_iota(jnp.int32, (c, c), 0)
    ci = lax.broadcasted_iota(jnp.int32, (c, c), 1)
    incl = ri >= ci
    strict = ri > ci

    def group_body(gi, carry):
        lanes = pl.ds(pl.multiple_of(gi * (group * c), group * c), group * c)
        gc_rows = [gcr_s[hb * hp + j, :, lanes] for j in range(hp)]
        units = [(j, g) for g in range(group) for j in range(hp)]
        rows_u, scores_u, decay_u, rhs_u, kdec_u, qdec_u = [], [], [], [], [], []
        for j, g in units:
            rows = pl.ds(pl.multiple_of((gi * group + g) * c, c), c)
            qc, kc, vc = q_s[j, rows, :], k_s[j, rows, :], v_s[j, rows, :]
            beta, gc = bc_s[j, rows, :], gc_s[j, rows, :]
            gc_row = gc_rows[j][:, g * c:(g + 1) * c]
            decay_u.append(jnp.where(incl, jnp.exp(jnp.where(incl, gc[:, :c] - gc_row, 0.0)), 0.0))
            k_beta = kc * beta
            e_gc = jnp.exp(gc)
            lhs = jnp.concatenate([k_beta.astype(BF16), qc.astype(BF16)], axis=0)
            scores_u.append(_dot_nt(lhs, kc.astype(BF16)))
            rhs_u.append(jnp.concatenate([(vc * beta).astype(BF16), (k_beta * e_gc).astype(BF16)], axis=1))
            kdec_u.append((kc * jnp.exp(gc[c - 1:c, :] - gc)).astype(BF16))
            qdec_u.append(qc * e_gc)
            rows_u.append(rows)
        ms = [jnp.where(strict, sc[:c] * dec, 0.0) for sc, dec in zip(scores_u, decay_u)]
        attn_u = [jnp.where(incl, sc[c:] * dec, 0.0).astype(BF16) for sc, dec in zip(scores_u, decay_u)]
        t_u = _unit_lower_inverses(ms)
        uw_u = [_dot(t.astype(BF16), rhs).astype(BF16) for t, rhs in zip(t_u, rhs_u)]
        bp_u = [_dot_tn(kdec, uw) for kdec, uw in zip(kdec_u, uw_u)]
        aw_u = [_dot(attn, uw) for attn, uw in zip(attn_u, uw_u)]
        for (j, g), rows, bp, aw, qdec in zip(units, rows_u, bp_u, aw_u, qdec_u):
            n = gi * group + g
            b_s[j, n] = bp[:, :hd]
            p_s[j, n] = bp[:, hd:].astype(BF16)
            qp_s[j, rows, :] = (qdec - aw[:, hd:]).astype(BF16)
            out_s[j, rows, :] = aw[:, :hd]
        return carry

    lax.fori_loop(0, s_len // (group * c), group_body, 0)

    def seq_body(n, states):
        r0 = pl.multiple_of(n * c, c)
        rows = pl.ds(r0, c)
        nxt = []
        for j in range(hp):
            s16 = states[j].astype(BF16)
            out_s[j, rows, :] += _dot(qp_s[j, rows, :], s16)
            gc_tail = gc_s[j, pl.ds(pl.multiple_of(r0 + c - 8, 8), 8), :]
            nxt.append(states[j] * jnp.exp(gc_tail[7:8, :]) - _dot(p_s[j, n], s16) + b_s[j, n])
        return tuple(nxt)

    lax.fori_loop(0, s_len // c, seq_body, tuple(jnp.zeros((hd, hd), F32) for _ in range(hp)), unroll=2)

    for j in range(hp):
        cols = slice(j * hd, (j + 1) * hd)
        o = _rmsnorm(out_s[j], onw_ref[...]) * _silu(z_ref[:, cols].astype(F32))
        o_ref[:, cols] = o.astype(o_ref.dtype)


def _gdn(proj, gates, conv_w, alog_pad, dtb_pad, out_norm_w, *, batch, n_heads, hp=GDN_HEADS_PER_STEP):
    t = proj.shape[0]
    s_len = t // batch
    hd = HEAD_DIM
    nblk = n_heads // hp
    n_chunks = s_len // GDN_CHUNK
    seq = lambda off: pl.BlockSpec((s_len, hp * hd), lambda b, h: (b, off + h))
    cw = lambda off: pl.BlockSpec((CONV_WIDTH, hp * hd), lambda b, h: (0, off + h))
    row = pl.BlockSpec((1, LANES), lambda b, h: (0, 0))
    act = pltpu.VMEM((s_len, LANES), F32)
    per_head = pltpu.VMEM((hp, s_len, hd), F32)
    return pl.pallas_call(
        functools.partial(_gdn_kernel, n_heads=n_heads, hp=hp),
        grid=(batch, nblk),
        in_specs=[seq(0), seq(nblk), seq(2 * nblk), seq(3 * nblk),
                  pl.BlockSpec((s_len, LANES), lambda b, h: (b, 0)),
                  cw(0), cw(nblk), cw(2 * nblk), row, row, row],
        out_specs=pl.BlockSpec((s_len, hp * hd), lambda b, h: (b, h)),
        out_shape=jax.ShapeDtypeStruct((t, n_heads * hd), BF16),
        scratch_shapes=[act, act, pltpu.VMEM((n_heads, 1, s_len), F32),
                        per_head, per_head, per_head, per_head, per_head,
                        pltpu.VMEM((hp, n_chunks, hd, hd), BF16), pltpu.VMEM((hp, n_chunks, hd, hd), F32),
                        pltpu.VMEM((hp, s_len, hd), BF16), per_head, pltpu.VMEM((CONV_PAD + s_len, hd), F32)],
        compiler_params=_cparams(2),
        name="gdn",
    )(proj, proj, proj, proj, gates, conv_w, conv_w, conv_w, alog_pad, dtb_pad, out_norm_w.reshape(1, hd))


def _gelu_tanh(x):
    return 0.5 * x * (1.0 + jnp.tanh(math.sqrt(2.0 / math.pi) * (x + 0.044715 * (x * x * x))))


def _lru_kernel(x_ref, y_ref, cw_ref, cb_ref, wa_ref, ba_ref, wx_ref, bx_ref, lam_ref, o_ref, pad_s):
    pad_s[:CONV_PAD, :] = jnp.zeros((CONV_PAD, pad_s.shape[1]), F32)
    xc = _causal_conv(x_ref[...].astype(F32), cw_ref[...], pad_s) + cb_ref[...]
    x16 = xc.astype(BF16)
    r = _sigmoid(_dot(x16, wa_ref[0]) + ba_ref[...])
    i = _sigmoid(_dot(x16, wx_ref[0]) + bx_ref[...])
    log_a = (-LRU_C) * r * _softplus(-lam_ref[...])
    a = jnp.exp(log_a)
    th = jnp.tanh(log_a)
    b = jnp.sqrt(-2.0 * th / (1.0 - th)) * i * xc
    row = lax.broadcasted_iota(jnp.int32, a.shape, 0)
    s_len = a.shape[0]
    sh = 1
    while sh < s_len:
        live = row >= sh
        b = a * jnp.where(live, pltpu.roll(b, sh, 0), 0.0) + b
        if sh * 2 < s_len:
            a = a * jnp.where(live, pltpu.roll(a, sh, 0), 1.0)
        sh *= 2
    o_ref[...] = (b * _gelu_tanh(y_ref[...].astype(F32))).astype(o_ref.dtype)


def _lru(proj, conv_w, conv_b, wa, ba, wx, bx, lam, *, batch, col_off):
    t = proj.shape[0]
    s_len = t // batch
    n_blocks, blk = wa.shape[0], wa.shape[1]
    seq = lambda off: pl.BlockSpec((s_len, blk), lambda b, n: (b, off + n))
    vec = pl.BlockSpec((1, blk), lambda b, n: (0, n))
    mat = pl.BlockSpec((1, blk, blk), lambda b, n: (n, 0, 0))
    width = n_blocks * blk
    return pl.pallas_call(
        _lru_kernel,
        grid=(batch, n_blocks),
        in_specs=[seq(col_off), seq(col_off + n_blocks),
                  pl.BlockSpec((CONV_WIDTH, blk), lambda b, n: (0, n)), vec, mat, vec, mat, vec, vec],
        out_specs=pl.BlockSpec((s_len, blk), lambda b, n: (b, n)),
        out_shape=jax.ShapeDtypeStruct((t, width), BF16),
        scratch_shapes=[pltpu.VMEM((CONV_PAD + s_len, blk), F32)],
        compiler_params=_cparams(2),
        name="rglru",
    )(proj, proj, conv_w, conv_b.reshape(1, width), wa, ba.reshape(1, width), wx, bx.reshape(1, width),
      lam.reshape(1, width))


LOG2E = math.log2(math.e)


def _softmax2_pv(logits2, v16):
    mx = jnp.max(logits2, axis=-1, keepdims=True)
    p = jnp.exp2(logits2 - mx)
    denom = jnp.sum(p, axis=-1, keepdims=True)
    return _dot(p.astype(BF16), v16) / denom


def _causal_tile(logits2, lo, extra=None):
    tq = logits2.shape[0]
    tri = lax.broadcasted_iota(jnp.int32, (tq, tq), 1) <= lax.broadcasted_iota(jnp.int32, (tq, tq), 0)
    parts = [] if lo == 0 else ([logits2[:, :lo]] if extra is None else extra)
    parts.append(jnp.where(tri, logits2[:, lo:], -jnp.inf))
    return parts[0] if len(parts) == 1 else jnp.concatenate(parts, axis=1)


def _fox_kernel(q_ref, k_ref, v_ref, g_ref, fb_ref, o_ref, cfc_s, cfr_s, *, n_heads):
    h = pl.program_id(1)
    s_len = q_ref.shape[0]
    tq = ATTN_Q_TILE

    @pl.when(h == 0)
    def _():
        z = g_ref[...] + fb_ref[...]
        log_f = -_softplus(-z)
        rows = _lane_cumsum(log_f.T[0:n_heads, :], s_len)
        for r in range(n_heads):
            cfr_s[r] = rows[r:r + 1, :]
        cfc_s[...] = _rows_to_columns(rows, LANES)

    cf_col2 = _lane_column(cfc_s[...], h) * LOG2E
    cf_row2 = cfr_s[h] * LOG2E
    k16 = k_ref[...]
    v16 = v_ref[...]
    scale2 = (HEAD_DIM ** -0.5) * LOG2E
    n_tiles = s_len // tq
    qk = lambda i: _dot_nt(q_ref[i * tq:(i + 1) * tq, :], k16[:(i + 1) * tq, :])
    qk_next = qk(0)
    for i in range(n_tiles):
        lo, hi = i * tq, (i + 1) * tq
        qk_cur, qk_next = qk_next, (qk(i + 1) if i + 1 < n_tiles else None)
        logits2 = qk_cur * scale2 + cf_col2[lo:hi, :] - cf_row2[:, :hi]
        o_ref[lo:hi, :] = _softmax2_pv(_causal_tile(logits2, lo), v16[:hi, :]).astype(o_ref.dtype)


def _fox(proj, gates, fb_pad, *, batch, n_heads):
    t = proj.shape[0]
    s_len = t // batch
    hd = HEAD_DIM
    seq = lambda off: pl.BlockSpec((s_len, hd), lambda b, h: (b, off + h))
    return pl.pallas_call(
        functools.partial(_fox_kernel, n_heads=n_heads),
        grid=(batch, n_heads),
        in_specs=[seq(0), seq(n_heads), seq(2 * n_heads),
                  pl.BlockSpec((s_len, LANES), lambda b, h: (b, 0)),
                  pl.BlockSpec((1, LANES), lambda b, h: (0, 0))],
        out_specs=pl.BlockSpec((s_len, hd), lambda b, h: (b, h)),
        out_shape=jax.ShapeDtypeStruct((t, n_heads * hd), BF16),
        scratch_shapes=[pltpu.VMEM((s_len, LANES), F32), pltpu.VMEM((n_heads, 1, s_len), F32)],
        compiler_params=_cparams(2),
        name="fox",
    )(proj, proj, proj, gates, fb_pad)


def _rotary(x, cos_t, sin_lo, sin_hi):
    half = ROPE_DIM // 2
    return x * cos_t + pltpu.roll(x, LANES - half, 1) * sin_lo + pltpu.roll(x, half, 1) * sin_hi


def _moba_kernel(q_ref, k_ref, v_ref, cos_ref, slo_ref, shi_ref, o_ref, q_s, sel_s):
    s_len = q_ref.shape[0]
    blk = MOBA_BLOCK
    nb = s_len // blk
    cos_t, sin_lo, sin_hi = cos_ref[...], slo_ref[...], shi_ref[...]
    q = _rotary(q_ref[...].astype(F32), cos_t, sin_lo, sin_hi)
    k = _rotary(k_ref[...].astype(F32), cos_t, sin_lo, sin_hi)
    q_s[...] = q.astype(BF16)
    k16 = k.astype(BF16)
    v16 = v_ref[...]

    nb_pad = -(-nb // 8) * 8
    k_mean = jnp.concatenate([jnp.mean(k[n * blk:(n + 1) * blk, :], axis=0, keepdims=True) for n in range(nb)]
                             + ([jnp.zeros((nb_pad - nb, HEAD_DIM), F32)] if nb_pad > nb else []), axis=0)
    gate = lax.dot_general(k_mean, q, (((1,), (1,)), ((), ())), preferred_element_type=F32,
                           precision=lax.Precision.HIGHEST)
    blk_row = lax.broadcasted_iota(jnp.int32, gate.shape, 0)
    q_blk = lax.broadcasted_iota(jnp.int32, gate.shape, 1) // blk
    past = blk_row < q_blk
    gate = jnp.where(past, gate, -jnp.inf)
    rank = jnp.zeros(gate.shape, F32)
    for m in range(nb - 1):
        g_m = gate[m:m + 1, :]
        ahead = (g_m > gate) | ((g_m == gate) & (blk_row > m))
        rank = rank + jnp.where(ahead & (q_blk > m), 1.0, 0.0)
    keep_bias = jnp.where(past & (rank < MOBA_TOPK), 0.0, -jnp.inf)
    sel_s[...] = _rows_to_columns(keep_bias, LANES)

    scale2 = (HEAD_DIM ** -0.5) * LOG2E
    for i in range(nb):
        lo, hi = i * blk, (i + 1) * blk
        logits2 = _dot_nt(q_s[lo:hi, :], k16[:hi, :]) * scale2
        sel = sel_s[lo:hi, :]
        extra = [logits2[:, j * blk:(j + 1) * blk] + sel[:, j:j + 1] for j in range(i)]
        o_ref[lo:hi, :] = _softmax2_pv(_causal_tile(logits2, lo, extra), v16[:hi, :]).astype(o_ref.dtype)


def _rope_tables(s_len):
    half = ROPE_DIM // 2
    inv_freq = ROPE_THETA ** (-jnp.arange(half, dtype=F32) / half)
    ang = jnp.arange(s_len, dtype=F32)[:, None] * inv_freq[None, :]
    cos, sin = jnp.cos(ang), jnp.sin(ang)
    zeros = jnp.zeros((s_len, HEAD_DIM - ROPE_DIM), F32)
    z_half = jnp.zeros((s_len, half), F32)
    cos_t = jnp.concatenate([cos, cos, zeros + 1.0], axis=1)
    sin_lo = jnp.concatenate([-sin, z_half, zeros], axis=1)
    sin_hi = jnp.concatenate([z_half, sin, zeros], axis=1)
    return cos_t, sin_lo, sin_hi


def _moba(proj, *, batch, n_heads, col_off):
    t = proj.shape[0]
    s_len = t // batch
    hd = HEAD_DIM
    seq = lambda off: pl.BlockSpec((s_len, hd), lambda b, h: (b, off + h))
    tab = pl.BlockSpec((s_len, hd), lambda b, h: (0, 0))
    return pl.pallas_call(
        _moba_kernel,
        grid=(batch, n_heads),
        in_specs=[seq(col_off), seq(col_off + n_heads), seq(col_off + 2 * n_heads), tab, tab, tab],
        out_specs=pl.BlockSpec((s_len, hd), lambda b, h: (b, h)),
        out_shape=jax.ShapeDtypeStruct((t, n_heads * hd), BF16),
        scratch_shapes=[pltpu.VMEM((s_len, hd), BF16), pltpu.VMEM((s_len, LANES), F32)],
        compiler_params=_cparams(2),
        name="moba",
    )(proj, proj, proj, *_rope_tables(s_len))


def _pad_lanes(parts):
    row = jnp.concatenate([p.reshape(1, -1).astype(F32) for p in parts], axis=1)
    return jnp.pad(row, ((0, 0), (0, LANES - row.shape[1])))


def _split_w_in(w, sizes, main_idx, gate_idx):
    offs = [0]
    for n in sizes:
        offs.append(offs[-1] + n)
    cols = lambda i: w[:, offs[i]:offs[i + 1]]
    main = jnp.concatenate([cols(i) for i in main_idx], axis=1).astype(BF16)
    gate = jnp.concatenate([cols(i) for i in gate_idx], axis=1)
    gate = jnp.pad(gate, ((0, 0), (0, LANES - gate.shape[1]))).astype(BF16)
    return main, gate


def kernel(x, ab_norm, ab_w_in, ab_conv_qkv, ab_a_log, ab_dt_bias, ab_out_norm, ab_lru_conv_w, ab_lru_conv_b,
           ab_lru_wa, ab_lru_ba, ab_lru_wx, ab_lru_bx, ab_lru_lambda, ab_w_out, cd_norm, cd_w_in, cd_f_bias,
           cd_w_out, ffn_norm, ffn_w_gate, ffn_w_up, ffn_w_down, final_norm):
    batch, s_len, d_model = x.shape
    depth = ffn_norm.shape[0]
    gdn_heads = ab_a_log.shape[1]
    gdn_width = gdn_heads * HEAD_DIM
    lru_width = ab_lru_lambda.shape[1]
    fox_heads = cd_f_bias.shape[1]
    fox_width = fox_heads * HEAD_DIM
    moba_width = d_model - fox_width
    moba_heads = moba_width // HEAD_DIM
    ab_sizes = (gdn_width,) * 4 + (gdn_heads,) * 2 + (lru_width,) * 2
    cd_sizes = (fox_width,) * 3 + (fox_heads,) + (moba_width,) * 3

    h = x.reshape(batch * s_len, d_model)
    for layer in range(depth):
        j = layer // 2
        if layer % 2 == 0:
            w_main, w_gate = _split_w_in(ab_w_in[j], ab_sizes, (0, 1, 2, 3, 6, 7), (4, 5))
            proj, gates = _inproj(h, ab_norm[j], w_main, w_gate)
            zeros_h = jnp.zeros((gdn_heads,), F32)
            o_a = _gdn(proj, gates, ab_conv_qkv[j], _pad_lanes([zeros_h, ab_a_log[j]]),
                       _pad_lanes([zeros_h, ab_dt_bias[j]]), ab_out_norm[j], batch=batch, n_heads=gdn_heads)
            o_b = _lru(proj, ab_lru_conv_w[j], ab_lru_conv_b[j], ab_lru_wa[j].astype(BF16), ab_lru_ba[j],
                       ab_lru_wx[j].astype(BF16), ab_lru_bx[j], ab_lru_lambda[j], batch=batch,
                       col_off=4 * gdn_heads)
            h = _outproj(h, o_a, o_b, ab_w_out[j].astype(BF16))
        else:
            w_main, w_gate = _split_w_in(cd_w_in[j], cd_sizes, (0, 1, 2, 4, 5, 6), (3,))
            proj, gates = _inproj(h, cd_norm[j], w_main, w_gate)
            o_c = _fox(proj, gates, _pad_lanes([cd_f_bias[j]]), batch=batch, n_heads=fox_heads)
            o_d = _moba(proj, batch=batch, n_heads=moba_heads, col_off=3 * fox_heads)
            h = _outproj(h, o_c, o_d, cd_w_out[j].astype(BF16))
        h = _ffn(h, ffn_norm[layer], ffn_w_gate[layer].astype(BF16), ffn_w_up[layer].astype(BF16),
                 ffn_w_down[layer].astype(BF16), final_norm, final_norm=(layer == depth - 1))
    return h.reshape(batch, s_len, d_model)
```

```python
import functools
import math

import jax
import jax.numpy as jnp
from jax import lax
from jax.experimental import pallas as pl
from jax.experimental.pallas import tpu as pltpu

F32 = jnp.float32
BF16 = jnp.bfloat16

HEAD_DIM = 128
LANES = 128
NORM_EPS = 1e-6
CONV_WIDTH = 4
GDN_CHUNK = 64
GDN_GROUP = 8
GDN_HEADS_PER_STEP = 2
LRU_C = 8.0
MOBA_BLOCK = 256
MOBA_TOPK = 3
ROPE_THETA = 500000.0
ROPE_DIM = HEAD_DIM // 4
ATTN_Q_TILE = 256
VMEM_LIMIT_BYTES = 48 * 1024 * 1024
FFN_VMEM_LIMIT_BYTES = 58 * 1024 * 1024


def _cparams(n_axes, vmem_limit_bytes=VMEM_LIMIT_BYTES):
    return pltpu.CompilerParams(dimension_semantics=("arbitrary",) * n_axes,
                                vmem_limit_bytes=vmem_limit_bytes)


def _dot(a, b):
    return jnp.dot(a, b, preferred_element_type=F32)


def _dot_nt(a, b):
    return lax.dot_general(a, b, (((1,), (1,)), ((), ())), preferred_element_type=F32)


def _dot_tn(a, b):
    return lax.dot_general(a, b, (((0,), (0,)), ((), ())), preferred_element_type=F32)


def _rmsnorm(x, w):
    ms = jnp.mean(x * x, axis=-1, keepdims=True)
    return x * lax.rsqrt(ms + NORM_EPS) * w


def _sigmoid(x):
    return 0.5 * jnp.tanh(0.5 * x) + 0.5


def _softplus(x):
    return jnp.maximum(x, 0.0) + jnp.log1p(jnp.exp(-jnp.abs(x)))


def _silu(x):
    half = 0.5 * x
    return half + half * jnp.tanh(half)


CONV_PAD = 8


def _causal_conv(x, w, pad_ref):
    s_len = x.shape[0]
    k = w.shape[0]
    pad_ref[CONV_PAD:, :] = x
    acc = x * w[k - 1:k, :]
    for j in range(k - 1):
        acc = acc + pad_ref[CONV_PAD - (k - 1 - j):CONV_PAD - (k - 1 - j) + s_len, :] * w[j:j + 1, :]
    return acc


def _lane_column(x, idx):
    lane = lax.broadcasted_iota(jnp.int32, x.shape, 1)
    return jnp.sum(jnp.where(lane == idx, x, 0.0), axis=-1, keepdims=True)


def _lane_cumsum(x, period):
    pos = lax.broadcasted_iota(jnp.int32, x.shape, 1) % period
    sh = 1
    while sh < period:
        x = x + jnp.where(pos >= sh, pltpu.roll(x, sh, 1), 0.0)
        sh *= 2
    return x


def _rows_to_columns(rows8, n_lanes):
    pad = jnp.zeros((n_lanes - rows8.shape[0], rows8.shape[1]), F32)
    return jnp.concatenate([rows8, pad], axis=0).T


def _inproj_kernel(x_ref, nw_ref, w_ref, wg_ref, o_ref, og_ref, xn_ref):
    @pl.when(pl.program_id(1) == 0)
    def _():
        xn = _rmsnorm(x_ref[...], nw_ref[...]).astype(BF16)
        xn_ref[...] = xn
        og_ref[...] = _dot(xn, wg_ref[...])

    o_ref[...] = _dot(xn_ref[...], w_ref[...]).astype(o_ref.dtype)


def _inproj(x, nw, w, wg, *, tm=1024, tn=1024):
    t, d = x.shape
    n = w.shape[1]
    return pl.pallas_call(
        _inproj_kernel,
        grid=(t // tm, n // tn),
        in_specs=[pl.BlockSpec((tm, d), lambda i, j: (i, 0)),
                  pl.BlockSpec((1, d), lambda i, j: (0, 0)),
                  pl.BlockSpec((d, tn), lambda i, j: (0, j)),
                  pl.BlockSpec((d, LANES), lambda i, j: (0, 0))],
        out_specs=[pl.BlockSpec((tm, tn), lambda i, j: (i, j)),
                   pl.BlockSpec((tm, LANES), lambda i, j: (i, 0))],
        out_shape=[jax.ShapeDtypeStruct((t, n), BF16), jax.ShapeDtypeStruct((t, LANES), F32)],
        scratch_shapes=[pltpu.VMEM((tm, d), BF16)],
        compiler_params=_cparams(2),
        name="inproj",
    )(x, nw.reshape(1, d), w, wg)


def _outproj_kernel(res_ref, a1_ref, a2_ref, w1_ref, w2_ref, o_ref):
    o_ref[...] = res_ref[...] + _dot(a1_ref[...], w1_ref[...]) + _dot(a2_ref[...], w2_ref[...])


def _outproj(res, a1, a2, w, *, tm=512, tn=2048):
    t, d = res.shape
    k1, k2 = a1.shape[1], a2.shape[1]
    assert k1 == k2 and w.shape == (k1 + k2, d)
    return pl.pallas_call(
        _outproj_kernel,
        grid=(t // tm, d // tn),
        in_specs=[pl.BlockSpec((tm, tn), lambda i, j: (i, j)),
                  pl.BlockSpec((tm, k1), lambda i, j: (i, 0)),
                  pl.BlockSpec((tm, k2), lambda i, j: (i, 0)),
                  pl.BlockSpec((k1, tn), lambda i, j: (0, j)),
                  pl.BlockSpec((k2, tn), lambda i, j: (1, j))],
        out_specs=pl.BlockSpec((tm, tn), lambda i, j: (i, j)),
        out_shape=jax.ShapeDtypeStruct((t, d), F32),
        compiler_params=_cparams(2),
        name="outproj",
    )(res, a1, a2, w, w)


def _ffn_kernel(x_ref, nw_ref, wg_ref, wu_ref, wd_ref, fnw_ref, o_ref, xn_ref, *, final_norm, down_chunk):
    f = pl.program_id(1)

    @pl.when(f == 0)
    def _():
        x = x_ref[...]
        xn_ref[...] = _rmsnorm(x, nw_ref[...]).astype(BF16)
        o_ref[...] = x

    xn = xn_ref[...]
    act = (_silu(_dot(xn, wg_ref[...])) * _dot(xn, wu_ref[...])).astype(BF16)
    for c0 in range(0, o_ref.shape[1], down_chunk):
        o_ref[:, c0:c0 + down_chunk] += _dot(act, wd_ref[:, c0:c0 + down_chunk])

    if final_norm:
        @pl.when(f == pl.num_programs(1) - 1)
        def _():
            o_ref[...] = _rmsnorm(o_ref[...], fnw_ref[...])


def _ffn(x, nw, wg, wu, wd, fnw, *, final_norm, tm=1024, tf=512, down_chunk=512):
    t, d = x.shape
    dff = wg.shape[1]
    return pl.pallas_call(
        functools.partial(_ffn_kernel, final_norm=final_norm, down_chunk=down_chunk),
        grid=(t // tm, dff // tf),
        in_specs=[pl.BlockSpec((tm, d), lambda i, f: (i, 0)),
                  pl.BlockSpec((1, d), lambda i, f: (0, 0)),
                  pl.BlockSpec((d, tf), lambda i, f: (0, f)),
                  pl.BlockSpec((d, tf), lambda i, f: (0, f)),
                  pl.BlockSpec((tf, d), lambda i, f: (f, 0)),
                  pl.BlockSpec((1, d), lambda i, f: (0, 0))],
        out_specs=pl.BlockSpec((tm, d), lambda i, f: (i, 0)),
        out_shape=jax.ShapeDtypeStruct((t, d), F32),
        scratch_shapes=[pltpu.VMEM((tm, d), BF16)],
        compiler_params=_cparams(2, FFN_VMEM_LIMIT_BYTES),
        name="ffn",
    )(x, nw.reshape(1, d), wg, wu, wd, fnw.reshape(1, d))


def _unit_lower_inverses(ms):
    c = ms[0].shape[0]
    eye = (lax.broadcasted_iota(jnp.int32, (c, c), 0) == lax.broadcasted_iota(jnp.int32, (c, c), 1)).astype(F32)
    invs = [eye - m for m in ms]
    m16s = [m.astype(BF16) for m in ms]
    ps = [_dot(m16, m16) for m16 in m16s]
    span = 2
    while span < c:
        p16s = [p.astype(BF16) for p in ps]
        if span * 2 < c:
            boths = [_dot(jnp.concatenate([inv.astype(BF16), p16], axis=0), p16) for inv, p16 in zip(invs, p16s)]
            invs = [inv + both[:c] for inv, both in zip(invs, boths)]
            ps = [both[c:] for both in boths]
        else:
            invs = [inv + _dot(inv.astype(BF16), p16) for inv, p16 in zip(invs, p16s)]
        span *= 2
    return invs


def _gdn_kernel(q_ref, k_ref, v_ref, z_ref, g_ref, cq_ref, ck_ref, cv_ref, alog_ref, dtb_ref, onw_ref,
                o_ref, beta_s, gcc_s, gcr_s, q_s, k_s, v_s, bc_s, gc_s, p_s, b_s, qp_s, out_s, pad_s,
                *, n_heads, hp):
    hb = pl.program_id(1)
    s_len = q_ref.shape[0]
    c = GDN_CHUNK
    group = GDN_GROUP
    hd = HEAD_DIM

    @pl.when(hb == 0)
    def _():
        logits = g_ref[...]
        beta_s[...] = _sigmoid(logits)
        log_decay = -jnp.exp(alog_ref[...]) * _softplus(logits + dtb_ref[...])
        rows = log_decay.T[n_heads:2 * n_heads, :]
        rows = _lane_cumsum(rows, c)
        for r in range(n_heads):
            gcr_s[r] = rows[r:r + 1, :]
        gcc_s[...] = _rows_to_columns(rows, LANES)

    pad_s[:CONV_PAD, :] = jnp.zeros((CONV_PAD, hd), F32)
    for j in range(hp):
        cols = slice(j * hd, (j + 1) * hd)
        head = hb * hp + j
        bc_s[j] = jnp.broadcast_to(_lane_column(beta_s[...], head), (s_len, hd))
        gc_s[j] = jnp.broadcast_to(_lane_column(gcc_s[...], head), (s_len, hd))
        q = _silu(_causal_conv(q_ref[:, cols].astype(F32), cq_ref[:, cols], pad_s))
        k = _silu(_causal_conv(k_ref[:, cols].astype(F32), ck_ref[:, cols], pad_s))
        v_s[j] = _silu(_causal_conv(v_ref[:, cols].astype(F32), cv_ref[:, cols], pad_s))
        q_s[j] = q * lax.rsqrt(jnp.sum(q * q, axis=-1, keepdims=True) + NORM_EPS) * (hd ** -0.5)
        k_s[j] = k * lax.rsqrt(jnp.sum(k * k, axis=-1, keepdims=True) + NORM_EPS)

    ri = lax.broadcasted_iota(jnp.int32, (c, c), 0)
    ci = lax.broadcasted_iota(jnp.int32, (c, c), 1)
    incl = ri >= ci
    strict = ri > ci

    def group_body(gi, carry):
        lanes = pl.ds(pl.multiple_of(gi * (group * c), group * c), group * c)
        gc_rows = [gcr_s[hb * hp + j, :, lanes] for j in range(hp)]
        units = [(j, g) for g in range(group) for j in range(hp)]
        rows_u, scores_u, decay_u, rhs_u, kdec_u, qdec_u = [], [], [], [], [], []
        for j, g in units:
            rows = pl.ds(pl.multiple_of((gi * group + g) * c, c), c)
            qc, kc, vc = q_s[j, rows, :], k_s[j, rows, :], v_s[j, rows, :]
            beta, gc = bc_s[j, rows, :], gc_s[j, rows, :]
            gc_row = gc_rows[j][:, g * c:(g + 1) * c]
            decay_u.append(jnp.where(incl, jnp.exp(jnp.where(incl, gc[:, :c] - gc_row, 0.0)), 0.0))
            k_beta = kc * beta
            e_gc = jnp.exp(gc)
            lhs = jnp.concatenate([k_beta.astype(BF16), qc.astype(BF16)], axis=0)
            scores_u.append(_dot_nt(lhs, kc.astype(BF16)))
            rhs_u.append(jnp.concatenate([(vc * beta).astype(BF16), (k_beta * e_gc).astype(BF16)], axis=1))
            kdec_u.append((kc * jnp.exp(gc[c - 1:c, :] - gc)).astype(BF16))
            qdec_u.append(qc * e_gc)
            rows_u.append(rows)
        ms = [jnp.where(strict, sc[:c] * dec, 0.0) for sc, dec in zip(scores_u, decay_u)]
        attn_u = [jnp.where(incl, sc[c:] * dec, 0.0).astype(BF16) for sc, dec in zip(scores_u, decay_u)]
        t_u = _unit_lower_inverses(ms)
        uw_u = [_dot(t.astype(BF16), rhs).astype(BF16) for t, rhs in zip(t_u, rhs_u)]
        bp_u = [_dot_tn(kdec, uw) for kdec, uw in zip(kdec_u, uw_u)]
        aw_u = [_dot(attn, uw) for attn, uw in zip(attn_u, uw_u)]
        for (j, g), rows, bp, aw, qdec in zip(units, rows_u, bp_u, aw_u, qdec_u):
            n = gi * group + g
            b_s[j, n] = bp[:, :hd]
            p_s[j, n] = bp[:, hd:].astype(BF16)
            qp_s[j, rows, :] = (qdec - aw[:, hd:]).astype(BF16)
            out_s[j, rows, :] = aw[:, :hd]
        return carry

    lax.fori_loop(0, s_len // (group * c), group_body, 0)

    def seq_body(n, states):
        r0 = pl.multiple_of(n * c, c)
        rows = pl.ds(r0, c)
        nxt = []
        for j in range(hp):
            s16 = states[j].astype(BF16)
            out_s[j, rows, :] += _dot(qp_s[j, rows, :], s16)
            gc_tail = gc_s[j, pl.ds(pl.multiple_of(r0 + c - 8, 8), 8), :]
            nxt.append(states[j] * jnp.exp(gc_tail[7:8, :]) - _dot(p_s[j, n], s16) + b_s[j, n])
        return tuple(nxt)

    lax.fori_loop(0, s_len // c, seq_body, tuple(jnp.zeros((hd, hd), F32) for _ in range(hp)), unroll=2)

    for j in range(hp):
        cols = slice(j * hd, (j + 1) * hd)
        o = _rmsnorm(out_s[j], onw_ref[...]) * _silu(z_ref[:, cols].astype(F32))
        o_ref[:, cols] = o.astype(o_ref.dtype)


def _gdn(proj, gates, conv_w, alog_pad, dtb_pad, out_norm_w, *, batch, n_heads, hp=GDN_HEADS_PER_STEP):
    t = proj.shape[0]
    s_len = t // batch
    hd = HEAD_DIM
    nblk = n_heads // hp
    n_chunks = s_len // GDN_CHUNK
    assert n_heads % hp == 0 and s_len % (GDN_GROUP * GDN_CHUNK) == 0
    seq = lambda off: pl.BlockSpec((s_len, hp * hd), lambda b, h: (b, off + h))
    cw = lambda off: pl.BlockSpec((CONV_WIDTH, hp * hd), lambda b, h: (0, off + h))
    row = pl.BlockSpec((1, LANES), lambda b, h: (0, 0))
    act = pltpu.VMEM((s_len, LANES), F32)
    per_head = pltpu.VMEM((hp, s_len, hd), F32)
    return pl.pallas_call(
        functools.partial(_gdn_kernel, n_heads=n_heads, hp=hp),
        grid=(batch, nblk),
        in_specs=[seq(0), seq(nblk), seq(2 * nblk), seq(3 * nblk),
                  pl.BlockSpec((s_len, LANES), lambda b, h: (b, 0)),
                  cw(0), cw(nblk), cw(2 * nblk), row, row, row],
        out_specs=pl.BlockSpec((s_len, hp * hd), lambda b, h: (b, h)),
        out_shape=jax.ShapeDtypeStruct((t, n_heads * hd), BF16),
        scratch_shapes=[act, act, pltpu.VMEM((n_heads, 1, s_len), F32),
                        per_head, per_head, per_head, per_head, per_head,
                        pltpu.VMEM((hp, n_chunks, hd, hd), BF16), pltpu.VMEM((hp, n_chunks, hd, hd), F32),
                        pltpu.VMEM((hp, s_len, hd), BF16), per_head, pltpu.VMEM((CONV_PAD + s_len, hd), F32)],
        compiler_params=_cparams(2),
        name="gdn",
    )(proj, proj, proj, proj, gates, conv_w, conv_w, conv_w, alog_pad, dtb_pad, out_norm_w.reshape(1, hd))


def _gelu_tanh(x):
    return 0.5 * x * (1.0 + jnp.tanh(math.sqrt(2.0 / math.pi) * (x + 0.044715 * (x * x * x))))


def _lru_kernel(x_ref, y_ref, cw_ref, cb_ref, wa_ref, ba_ref, wx_ref, bx_ref, lam_ref, o_ref, pad_s):
    pad_s[:CONV_PAD, :] = jnp.zeros((CONV_PAD, pad_s.shape[1]), F32)
    xc = _causal_conv(x_ref[...].astype(F32), cw_ref[...], pad_s) + cb_ref[...]
    x16 = xc.astype(BF16)
    r = _sigmoid(_dot(x16, wa_ref[0]) + ba_ref[...])
    i = _sigmoid(_dot(x16, wx_ref[0]) + bx_ref[...])
    log_a = (-LRU_C) * r * _softplus(-lam_ref[...])
    a = jnp.exp(log_a)
    th = jnp.tanh(log_a)
    b = jnp.sqrt(-2.0 * th / (1.0 - th)) * i * xc
    row = lax.broadcasted_iota(jnp.int32, a.shape, 0)
    s_len, width = a.shape
    sublanes = 8

    def shifted(x, sh, fill):
        if sh % sublanes == 0:
            return jnp.concatenate([jnp.full((sh, width), fill, F32), x[:s_len - sh, :]], axis=0)
        return jnp.where(row >= sh, pltpu.roll(x, sh, 0), fill)

    sh = 1
    while sh < s_len:
        b = a * shifted(b, sh, 0.0) + b
        if sh * 2 < s_len:
            a = a * shifted(a, sh, 1.0)
        sh *= 2
    o_ref[...] = (b * _gelu_tanh(y_ref[...].astype(F32))).astype(o_ref.dtype)


def _lru(proj, conv_w, conv_b, wa, ba, wx, bx, lam, *, batch, col_off):
    t = proj.shape[0]
    s_len = t // batch
    n_blocks, blk = wa.shape[0], wa.shape[1]
    seq = lambda off: pl.BlockSpec((s_len, blk), lambda b, n: (b, off + n))
    vec = pl.BlockSpec((1, blk), lambda b, n: (0, n))
    mat = pl.BlockSpec((1, blk, blk), lambda b, n: (n, 0, 0))
    width = n_blocks * blk
    return pl.pallas_call(
        _lru_kernel,
        grid=(batch, n_blocks),
        in_specs=[seq(col_off), seq(col_off + n_blocks),
                  pl.BlockSpec((CONV_WIDTH, blk), lambda b, n: (0, n)), vec, mat, vec, mat, vec, vec],
        out_specs=pl.BlockSpec((s_len, blk), lambda b, n: (b, n)),
        out_shape=jax.ShapeDtypeStruct((t, width), BF16),
        scratch_shapes=[pltpu.VMEM((CONV_PAD + s_len, blk), F32)],
        compiler_params=_cparams(2),
        name="rglru",
    )(proj, proj, conv_w, conv_b.reshape(1, width), wa, ba.reshape(1, width), wx, bx.reshape(1, width),
      lam.reshape(1, width))


LOG2E = math.log2(math.e)


def _softmax2_pv(logits2, v16):
    mx = jnp.max(logits2, axis=-1, keepdims=True)
    p = jnp.exp2(logits2 - mx)
    denom = jnp.sum(p, axis=-1, keepdims=True)
    return _dot(p.astype(BF16), v16) / denom


def _causal_tile(logits2, lo, extra=None):
    tq = logits2.shape[0]
    tri = lax.broadcasted_iota(jnp.int32, (tq, tq), 1) <= lax.broadcasted_iota(jnp.int32, (tq, tq), 0)
    parts = [] if lo == 0 else ([logits2[:, :lo]] if extra is None else extra)
    parts.append(jnp.where(tri, logits2[:, lo:], -jnp.inf))
    return parts[0] if len(parts) == 1 else jnp.concatenate(parts, axis=1)


def _fox_kernel(q_ref, k_ref, v_ref, g_ref, fb_ref, o_ref, cfc_s, cfr_s, *, n_heads):
    h = pl.program_id(1)
    s_len = q_ref.shape[0]
    tq = ATTN_Q_TILE

    @pl.when(h == 0)
    def _():
        z = g_ref[...] + fb_ref[...]
        log_f = -_softplus(-z)
        rows = _lane_cumsum(log_f.T[0:n_heads, :], s_len)
        for r in range(n_heads):
            cfr_s[r] = rows[r:r + 1, :]
        cfc_s[...] = _rows_to_columns(rows, LANES)

    cf_col2 = _lane_column(cfc_s[...], h) * LOG2E
    cf_row2 = cfr_s[h] * LOG2E
    k16 = k_ref[...]
    v16 = v_ref[...]
    scale2 = (HEAD_DIM ** -0.5) * LOG2E
    n_tiles = s_len // tq
    qk = lambda i: _dot_nt(q_ref[i * tq:(i + 1) * tq, :], k16[:(i + 1) * tq, :])
    qk_next = qk(0)
    for i in range(n_tiles):
        lo, hi = i * tq, (i + 1) * tq
        qk_cur, qk_next = qk_next, (qk(i + 1) if i + 1 < n_tiles else None)
        logits2 = qk_cur * scale2 + cf_col2[lo:hi, :] - cf_row2[:, :hi]
        o_ref[lo:hi, :] = _softmax2_pv(_causal_tile(logits2, lo), v16[:hi, :]).astype(o_ref.dtype)


def _fox(proj, gates, fb_pad, *, batch, n_heads):
    t = proj.shape[0]
    s_len = t // batch
    hd = HEAD_DIM
    seq = lambda off: pl.BlockSpec((s_len, hd), lambda b, h: (b, off + h))
    return pl.pallas_call(
        functools.partial(_fox_kernel, n_heads=n_heads),
        grid=(batch, n_heads),
        in_specs=[seq(0), seq(n_heads), seq(2 * n_heads),
                  pl.BlockSpec((s_len, LANES), lambda b, h: (b, 0)),
                  pl.BlockSpec((1, LANES), lambda b, h: (0, 0))],
        out_specs=pl.BlockSpec((s_len, hd), lambda b, h: (b, h)),
        out_shape=jax.ShapeDtypeStruct((t, n_heads * hd), BF16),
        scratch_shapes=[pltpu.VMEM((s_len, LANES), F32), pltpu.VMEM((n_heads, 1, s_len), F32)],
        compiler_params=_cparams(2),
        name="fox",
    )(proj, proj, proj, gates, fb_pad)


def _rotary(x, cos_t, sin_lo, sin_hi):
    half = ROPE_DIM // 2
    return x * cos_t + pltpu.roll(x, LANES - half, 1) * sin_lo + pltpu.roll(x, half, 1) * sin_hi


def _moba_kernel(q_ref, k_ref, v_ref, cos_ref, slo_ref, shi_ref, o_ref, q_s, sel_s):
    s_len = q_ref.shape[0]
    blk = MOBA_BLOCK
    nb = s_len // blk
    cos_t, sin_lo, sin_hi = cos_ref[...], slo_ref[...], shi_ref[...]
    q = _rotary(q_ref[...].astype(F32), cos_t, sin_lo, sin_hi)
    k = _rotary(k_ref[...].astype(F32), cos_t, sin_lo, sin_hi)
    q_s[...] = q.astype(BF16)
    k16 = k.astype(BF16)
    v16 = v_ref[...]

    nb_pad = -(-nb // 8) * 8
    k_mean = jnp.concatenate([jnp.mean(k[n * blk:(n + 1) * blk, :], axis=0, keepdims=True) for n in range(nb)]
                             + ([jnp.zeros((nb_pad - nb, HEAD_DIM), F32)] if nb_pad > nb else []), axis=0)
    gate = lax.dot_general(k_mean, q, (((1,), (1,)), ((), ())), preferred_element_type=F32,
                           precision=lax.Precision.HIGHEST)
    blk_row = lax.broadcasted_iota(jnp.int32, gate.shape, 0)
    q_blk = lax.broadcasted_iota(jnp.int32, gate.shape, 1) // blk
    past = blk_row < q_blk
    gate = jnp.where(past, gate, -jnp.inf)
    rank = jnp.zeros(gate.shape, F32)
    for m in range(nb - 1):
        g_m = gate[m:m + 1, :]
        ahead = (g_m > gate) | ((g_m == gate) & (blk_row > m))
        rank = rank + jnp.where(ahead & (q_blk > m), 1.0, 0.0)
    keep_bias = jnp.where(past & (rank < MOBA_TOPK), 0.0, -jnp.inf)
    sel_s[...] = _rows_to_columns(keep_bias, LANES)

    scale2 = (HEAD_DIM ** -0.5) * LOG2E
    for i in range(nb):
        lo, hi = i * blk, (i + 1) * blk
        logits2 = _dot_nt(q_s[lo:hi, :], k16[:hi, :]) * scale2
        sel = sel_s[lo:hi, :]
        extra = [logits2[:, j * blk:(j + 1) * blk] + sel[:, j:j + 1] for j in range(i)]
        o_ref[lo:hi, :] = _softmax2_pv(_causal_tile(logits2, lo, extra), v16[:hi, :]).astype(o_ref.dtype)


def _rope_tables(s_len):
    half = ROPE_DIM // 2
    inv_freq = ROPE_THETA ** (-jnp.arange(half, dtype=F32) / half)
    ang = jnp.arange(s_len, dtype=F32)[:, None] * inv_freq[None, :]
    cos, sin = jnp.cos(ang), jnp.sin(ang)
    zeros = jnp.zeros((s_len, HEAD_DIM - ROPE_DIM), F32)
    z_half = jnp.zeros((s_len, half), F32)
    cos_t = jnp.concatenate([cos, cos, zeros + 1.0], axis=1)
    sin_lo = jnp.concatenate([-sin, z_half, zeros], axis=1)
    sin_hi = jnp.concatenate([z_half, sin, zeros], axis=1)
    return cos_t, sin_lo, sin_hi


def _moba(proj, *, batch, n_heads, col_off):
    t = proj.shape[0]
    s_len = t // batch
    hd = HEAD_DIM
    seq = lambda off: pl.BlockSpec((s_len, hd), lambda b, h: (b, off + h))
    tab = pl.BlockSpec((s_len, hd), lambda b, h: (0, 0))
    return pl.pallas_call(
        _moba_kernel,
        grid=(batch, n_heads),
        in_specs=[seq(col_off), seq(col_off + n_heads), seq(col_off + 2 * n_heads), tab, tab, tab],
        out_specs=pl.BlockSpec((s_len, hd), lambda b, h: (b, h)),
        out_shape=jax.ShapeDtypeStruct((t, n_heads * hd), BF16),
        scratch_shapes=[pltpu.VMEM((s_len, hd), BF16), pltpu.VMEM((s_len, LANES), F32)],
        compiler_params=_cparams(2),
        name="moba",
    )(proj, proj, proj, *_rope_tables(s_len))


def _pad_lanes(parts):
    row = jnp.concatenate([p.reshape(1, -1).astype(F32) for p in parts], axis=1)
    return jnp.pad(row, ((0, 0), (0, LANES - row.shape[1])))


def _regroup_kernel(w_ref, main_ref, gate_ref, *, main_runs, gate_runs):
    w = w_ref[...]
    main_ref[...] = jnp.concatenate([w[:, a:b] for a, b in main_runs], axis=1).astype(main_ref.dtype)
    gate = [w[:, a:b] for a, b in gate_runs]
    n_gate = sum(b - a for a, b in gate_runs)
    gate.append(jnp.zeros((w.shape[0], gate_ref.shape[1] - n_gate), w.dtype))
    gate_ref[...] = jnp.concatenate(gate, axis=1).astype(gate_ref.dtype)


def _split_w_in(w, sizes, main_idx, gate_idx, *, tr=256):
    offs = [0]
    for n in sizes:
        offs.append(offs[-1] + n)

    def runs(idx):
        out = []
        for i in idx:
            if out and out[-1][1] == offs[i]:
                out[-1] = (out[-1][0], offs[i + 1])
            else:
                out.append((offs[i], offs[i + 1]))
        return tuple(out)

    d, n_in = w.shape
    n_main = sum(sizes[i] for i in main_idx)
    return pl.pallas_call(
        functools.partial(_regroup_kernel, main_runs=runs(main_idx), gate_runs=runs(gate_idx)),
        grid=(d // tr,),
        in_specs=[pl.BlockSpec((tr, n_in), lambda i: (i, 0))],
        out_specs=[pl.BlockSpec((tr, n_main), lambda i: (i, 0)), pl.BlockSpec((tr, LANES), lambda i: (i, 0))],
        out_shape=[jax.ShapeDtypeStruct((d, n_main), BF16), jax.ShapeDtypeStruct((d, LANES), BF16)],
        compiler_params=_cparams(1),
        name="regroup_w_in",
    )(w)


def kernel(x, ab_norm, ab_w_in, ab_conv_qkv, ab_a_log, ab_dt_bias, ab_out_norm, ab_lru_conv_w, ab_lru_conv_b,
           ab_lru_wa, ab_lru_ba, ab_lru_wx, ab_lru_bx, ab_lru_lambda, ab_w_out, cd_norm, cd_w_in, cd_f_bias,
           cd_w_out, ffn_norm, ffn_w_gate, ffn_w_up, ffn_w_down, final_norm):
    batch, s_len, d_model = x.shape
    depth = ffn_norm.shape[0]
    gdn_heads = ab_a_log.shape[1]
    gdn_width = gdn_heads * HEAD_DIM
    lru_width = ab_lru_lambda.shape[1]
    fox_heads = cd_f_bias.shape[1]
    fox_width = fox_heads * HEAD_DIM
    moba_width = d_model - fox_width
    moba_heads = moba_width // HEAD_DIM
    ab_sizes = (gdn_width,) * 4 + (gdn_heads,) * 2 + (lru_width,) * 2
    cd_sizes = (fox_width,) * 3 + (fox_heads,) + (moba_width,) * 3

    h = x.reshape(batch * s_len, d_model)
    for layer in range(depth):
        j = layer // 2
        if layer % 2 == 0:
            w_main, w_gate = _split_w_in(ab_w_in[j], ab_sizes, (0, 1, 2, 3, 6, 7), (4, 5))
            proj, gates = _inproj(h, ab_norm[j], w_main, w_gate)
            zeros_h = jnp.zeros((gdn_heads,), F32)
            o_a = _gdn(proj, gates, ab_conv_qkv[j], _pad_lanes([zeros_h, ab_a_log[j]]),
                       _pad_lanes([zeros_h, ab_dt_bias[j]]), ab_out_norm[j], batch=batch, n_heads=gdn_heads)
            o_b = _lru(proj, ab_lru_conv_w[j], ab_lru_conv_b[j], ab_lru_wa[j].astype(BF16), ab_lru_ba[j],
                       ab_lru_wx[j].astype(BF16), ab_lru_bx[j], ab_lru_lambda[j], batch=batch,
                       col_off=4 * gdn_heads)
            h = _outproj(h, o_a, o_b, ab_w_out[j].astype(BF16))
        else:
            w_main, w_gate = _split_w_in(cd_w_in[j], cd_sizes, (0, 1, 2, 4, 5, 6), (3,))
            proj, gates = _inproj(h, cd_norm[j], w_main, w_gate)
            o_c = _fox(proj, gates, _pad_lanes([cd_f_bias[j]]), batch=batch, n_heads=fox_heads)
            o_d = _moba(proj, batch=batch, n_heads=moba_heads, col_off=3 * fox_heads)
            h = _outproj(h, o_c, o_d, cd_w_out[j].astype(BF16))
        h = _ffn(h, ffn_norm[layer], ffn_w_gate[layer].astype(BF16), ffn_w_up[layer].astype(BF16),
                 ffn_w_down[layer].astype(BF16), final_norm, final_norm=(layer == depth - 1))
    return h.reshape(batch, s_len, d_model)
```

```python
import functools
import math

import jax
import jax.numpy as jnp
from jax import lax
from jax.experimental import pallas as pl
from jax.experimental.pallas import tpu as pltpu

F32 = jnp.float32
BF16 = jnp.bfloat16

HEAD_DIM = 128
LANES = 128
NORM_EPS = 1e-6
CONV_WIDTH = 4
GDN_CHUNK = 64
GDN_GROUP = 8
GDN_HEADS_PER_STEP = 2
LRU_C = 8.0
MOBA_BLOCK = 256
MOBA_TOPK = 3
ROPE_THETA = 500000.0
ROPE_DIM = HEAD_DIM // 4
ATTN_Q_TILE = 256
VMEM_LIMIT_BYTES = 48 * 1024 * 1024
FFN_VMEM_LIMIT_BYTES = 58 * 1024 * 1024


def _cparams(n_axes, vmem_limit_bytes=VMEM_LIMIT_BYTES):
    return pltpu.CompilerParams(dimension_semantics=("arbitrary",) * n_axes,
                                vmem_limit_bytes=vmem_limit_bytes)


def _dot(a, b):
    return jnp.dot(a, b, preferred_element_type=F32)


def _dot_nt(a, b):
    return lax.dot_general(a, b, (((1,), (1,)), ((), ())), preferred_element_type=F32)


def _dot_tn(a, b):
    return lax.dot_general(a, b, (((0,), (0,)), ((), ())), preferred_element_type=F32)


def _rmsnorm(x, w):
    ms = jnp.mean(x * x, axis=-1, keepdims=True)
    return x * lax.rsqrt(ms + NORM_EPS) * w


def _sigmoid(x):
    return 0.5 * jnp.tanh(0.5 * x) + 0.5


def _softplus(x):
    return jnp.maximum(x, 0.0) + jnp.log1p(jnp.exp(-jnp.abs(x)))


def _silu(x):
    half = 0.5 * x
    return half + half * jnp.tanh(half)


CONV_PAD = 8


def _causal_conv(x, w, pad_ref):
    s_len = x.shape[0]
    k = w.shape[0]
    pad_ref[CONV_PAD:, :] = x
    acc = x * w[k - 1:k, :]
    for j in range(k - 1):
        acc = acc + pad_ref[CONV_PAD - (k - 1 - j):CONV_PAD - (k - 1 - j) + s_len, :] * w[j:j + 1, :]
    return acc


def _lane_column(x, idx):
    lane = lax.broadcasted_iota(jnp.int32, x.shape, 1)
    return jnp.sum(jnp.where(lane == idx, x, 0.0), axis=-1, keepdims=True)


def _lane_cumsum(x, period):
    pos = lax.broadcasted_iota(jnp.int32, x.shape, 1) % period
    sh = 1
    while sh < period:
        x = x + jnp.where(pos >= sh, pltpu.roll(x, sh, 1), 0.0)
        sh *= 2
    return x


def _rows_to_columns(rows8, n_lanes):
    pad = jnp.zeros((n_lanes - rows8.shape[0], rows8.shape[1]), F32)
    return jnp.concatenate([rows8, pad], axis=0).T


def _inproj_kernel(x_ref, nw_ref, w_ref, wg_ref, o_ref, og_ref, xn_ref):
    @pl.when(pl.program_id(1) == 0)
    def _():
        xn = _rmsnorm(x_ref[...], nw_ref[...]).astype(BF16)
        xn_ref[...] = xn
        og_ref[...] = _dot(xn, wg_ref[...])

    o_ref[...] = _dot(xn_ref[...], w_ref[...]).astype(o_ref.dtype)


def _inproj(x, nw, w, wg, *, tm=1024, tn=1024):
    t, d = x.shape
    n = w.shape[1]
    return pl.pallas_call(
        _inproj_kernel,
        grid=(t // tm, n // tn),
        in_specs=[pl.BlockSpec((tm, d), lambda i, j: (i, 0)),
                  pl.BlockSpec((1, d), lambda i, j: (0, 0)),
                  pl.BlockSpec((d, tn), lambda i, j: (0, j)),
                  pl.BlockSpec((d, LANES), lambda i, j: (0, 0))],
        out_specs=[pl.BlockSpec((tm, tn), lambda i, j: (i, j)),
                   pl.BlockSpec((tm, LANES), lambda i, j: (i, 0))],
        out_shape=[jax.ShapeDtypeStruct((t, n), BF16), jax.ShapeDtypeStruct((t, LANES), F32)],
        scratch_shapes=[pltpu.VMEM((tm, d), BF16)],
        compiler_params=_cparams(2),
        name="inproj",
    )(x, nw.reshape(1, d), w, wg)


def _outproj_kernel(res_ref, a1_ref, a2_ref, w1_ref, w2_ref, o_ref):
    o_ref[...] = res_ref[...] + _dot(a1_ref[...], w1_ref[...]) + _dot(a2_ref[...], w2_ref[...])


def _outproj(res, a1, a2, w, *, tm=512, tn=2048):
    t, d = res.shape
    k1, k2 = a1.shape[1], a2.shape[1]
    assert k1 == k2 and w.shape == (k1 + k2, d)
    return pl.pallas_call(
        _outproj_kernel,
        grid=(t // tm, d // tn),
        in_specs=[pl.BlockSpec((tm, tn), lambda i, j: (i, j)),
                  pl.BlockSpec((tm, k1), lambda i, j: (i, 0)),
                  pl.BlockSpec((tm, k2), lambda i, j: (i, 0)),
                  pl.BlockSpec((k1, tn), lambda i, j: (0, j)),
                  pl.BlockSpec((k2, tn), lambda i, j: (1, j))],
        out_specs=pl.BlockSpec((tm, tn), lambda i, j: (i, j)),
        out_shape=jax.ShapeDtypeStruct((t, d), F32),
        compiler_params=_cparams(2),
        name="outproj",
    )(res, a1, a2, w, w)


def _ffn_kernel(x_ref, nw_ref, wg_ref, wu_ref, wd_ref, fnw_ref, o_ref, xn_ref, *, final_norm, down_chunk):
    f = pl.program_id(1)

    @pl.when(f == 0)
    def _():
        x = x_ref[...]
        xn_ref[...] = _rmsnorm(x, nw_ref[...]).astype(BF16)
        o_ref[...] = x

    xn = xn_ref[...]
    act = (_silu(_dot(xn, wg_ref[...])) * _dot(xn, wu_ref[...])).astype(BF16)
    for c0 in range(0, o_ref.shape[1], down_chunk):
        o_ref[:, c0:c0 + down_chunk] += _dot(act, wd_ref[:, c0:c0 + down_chunk])

    if final_norm:
        @pl.when(f == pl.num_programs(1) - 1)
        def _():
            o_ref[...] = _rmsnorm(o_ref[...], fnw_ref[...])


def _ffn(x, nw, wg, wu, wd, fnw, *, layer, final_norm, tm=1024, tf=512, down_chunk=512):
    t, d = x.shape
    dff = wg.shape[2]
    return pl.pallas_call(
        functools.partial(_ffn_kernel, final_norm=final_norm, down_chunk=down_chunk),
        grid=(t // tm, dff // tf),
        in_specs=[pl.BlockSpec((tm, d), lambda i, f: (i, 0)),
                  pl.BlockSpec((1, d), lambda i, f: (0, 0)),
                  pl.BlockSpec((None, d, tf), lambda i, f: (layer, 0, f)),
                  pl.BlockSpec((None, d, tf), lambda i, f: (layer, 0, f)),
                  pl.BlockSpec((None, tf, d), lambda i, f: (layer, f, 0)),
                  pl.BlockSpec((1, d), lambda i, f: (0, 0))],
        out_specs=pl.BlockSpec((tm, d), lambda i, f: (i, 0)),
        out_shape=jax.ShapeDtypeStruct((t, d), F32),
        scratch_shapes=[pltpu.VMEM((tm, d), BF16)],
        compiler_params=_cparams(2, FFN_VMEM_LIMIT_BYTES),
        name="ffn",
    )(x, nw.reshape(1, d), wg, wu, wd, fnw.reshape(1, d))


def _unit_lower_inverses(ms):
    c = ms[0].shape[0]
    eye = (lax.broadcasted_iota(jnp.int32, (c, c), 0) == lax.broadcasted_iota(jnp.int32, (c, c), 1)).astype(F32)
    invs = [eye - m for m in ms]
    m16s = [m.astype(BF16) for m in ms]
    ps = [_dot(m16, m16) for m16 in m16s]
    span = 2
    while span < c:
        p16s = [p.astype(BF16) for p in ps]
        if span * 2 < c:
            boths = [_dot(jnp.concatenate([inv.astype(BF16), p16], axis=0), p16) for inv, p16 in zip(invs, p16s)]
            invs = [inv + both[:c] for inv, both in zip(invs, boths)]
            ps = [both[c:] for both in boths]
        else:
            invs = [inv + _dot(inv.astype(BF16), p16) for inv, p16 in zip(invs, p16s)]
        span *= 2
    return invs


def _gdn_kernel(q_ref, k_ref, v_ref, z_ref, g_ref, cq_ref, ck_ref, cv_ref, alog_ref, dtb_ref, onw_ref,
                o_ref, beta_s, gcc_s, gcr_s, q_s, k_s, v_s, bc_s, gc_s, p_s, b_s, qp_s, out_s, pad_s,
                *, n_heads, hp):
    hb = pl.program_id(1)
    s_len = q_ref.shape[0]
    c = GDN_CHUNK
    group = GDN_GROUP
    hd = HEAD_DIM

    @pl.when(hb == 0)
    def _():
        logits = g_ref[...]
        beta_s[...] = _sigmoid(logits)
        log_decay = -jnp.exp(alog_ref[...]) * _softplus(logits + dtb_ref[...])
        rows = log_decay.T[n_heads:2 * n_heads, :]
        rows = _lane_cumsum(rows, c)
        for r in range(n_heads):
            gcr_s[r] = rows[r:r + 1, :]
        gcc_s[...] = _rows_to_columns(rows, LANES)

    pad_s[:CONV_PAD, :] = jnp.zeros((CONV_PAD, hd), F32)
    for j in range(hp):
        cols = slice(j * hd, (j + 1) * hd)
        head = hb * hp + j
        bc_s[j] = jnp.broadcast_to(_lane_column(beta_s[...], head), (s_len, hd))
        gc_s[j] = jnp.broadcast_to(_lane_column(gcc_s[...], head), (s_len, hd))
        q = _silu(_causal_conv(q_ref[:, cols].astype(F32), cq_ref[:, cols], pad_s))
        k = _silu(_causal_conv(k_ref[:, cols].astype(F32), ck_ref[:, cols], pad_s))
        v_s[j] = _silu(_causal_conv(v_ref[:, cols].astype(F32), cv_ref[:, cols], pad_s))
        q_s[j] = q * (lax.rsqrt(jnp.sum(q * q, axis=-1, keepdims=True) + NORM_EPS) * (hd ** -0.5))
        k_s[j] = k * lax.rsqrt(jnp.sum(k * k, axis=-1, keepdims=True) + NORM_EPS)

    ri = lax.broadcasted_iota(jnp.int32, (c, c), 0)
    ci = lax.broadcasted_iota(jnp.int32, (c, c), 1)
    incl = ri >= ci
    strict = ri > ci

    def group_body(gi, carry):
        lanes = pl.ds(pl.multiple_of(gi * (group * c), group * c), group * c)
        gc_rows = [gcr_s[hb * hp + j, :, lanes] for j in range(hp)]
        units = [(j, g) for g in range(group) for j in range(hp)]
        rows_u, scores_u, decay_u, rhs_u, kdec_u, qdec_u = [], [], [], [], [], []
        for j, g in units:
            rows = pl.ds(pl.multiple_of((gi * group + g) * c, c), c)
            qc, kc, vc = q_s[j, rows, :], k_s[j, rows, :], v_s[j, rows, :]
            beta, gc = bc_s[j, rows, :], gc_s[j, rows, :]
            gc_row = gc_rows[j][:, g * c:(g + 1) * c]
            decay_u.append(jnp.where(incl, jnp.exp(jnp.where(incl, gc[:, :c] - gc_row, 0.0)), 0.0))
            k_beta = kc * beta
            e_gc = jnp.exp(gc)
            lhs = jnp.concatenate([k_beta.astype(BF16), qc.astype(BF16)], axis=0)
            scores_u.append(_dot_nt(lhs, kc.astype(BF16)))
            rhs_u.append(jnp.concatenate([(vc * beta).astype(BF16), (k_beta * e_gc).astype(BF16)], axis=1))
            kdec_u.append((kc * jnp.exp(gc[c - 1:c, :] - gc)).astype(BF16))
            qdec_u.append(qc * e_gc)
            rows_u.append(rows)
        ms = [jnp.where(strict, sc[:c] * dec, 0.0) for sc, dec in zip(scores_u, decay_u)]
        attn_u = [jnp.where(incl, sc[c:] * dec, 0.0).astype(BF16) for sc, dec in zip(scores_u, decay_u)]
        t_u = _unit_lower_inverses(ms)
        uw_u = [_dot(t.astype(BF16), rhs).astype(BF16) for t, rhs in zip(t_u, rhs_u)]
        bp_u = [_dot_tn(kdec, uw) for kdec, uw in zip(kdec_u, uw_u)]
        aw_u = [_dot(attn, uw) for attn, uw in zip(attn_u, uw_u)]
        for (j, g), rows, bp, aw, qdec in zip(units, rows_u, bp_u, aw_u, qdec_u):
            n = gi * group + g
            b_s[j, n] = bp[:, :hd]
            p_s[j, n] = bp[:, hd:].astype(BF16)
            qp_s[j, rows, :] = (qdec - aw[:, hd:]).astype(BF16)
            out_s[j, rows, :] = aw[:, :hd]
        return carry

    lax.fori_loop(0, s_len // (group * c), group_body, 0)

    def seq_body(n, states):
        r0 = pl.multiple_of(n * c, c)
        rows = pl.ds(r0, c)
        nxt = []
        for j in range(hp):
            s16 = states[j].astype(BF16)
            out_s[j, rows, :] += _dot(qp_s[j, rows, :], s16)
            gc_tail = gc_s[j, pl.ds(pl.multiple_of(r0 + c - 8, 8), 8), :]
            nxt.append(states[j] * jnp.exp(gc_tail[7:8, :]) - _dot(p_s[j, n], s16) + b_s[j, n])
        return tuple(nxt)

    lax.fori_loop(0, s_len // c, seq_body, tuple(jnp.zeros((hd, hd), F32) for _ in range(hp)), unroll=2)

    for j in range(hp):
        cols = slice(j * hd, (j + 1) * hd)
        o = _rmsnorm(out_s[j], onw_ref[...]) * _silu(z_ref[:, cols].astype(F32))
        o_ref[:, cols] = o.astype(o_ref.dtype)


def _gdn(proj, gates, conv_w, alog_pad, dtb_pad, out_norm_w, *, batch, n_heads, hp=GDN_HEADS_PER_STEP):
    t = proj.shape[0]
    s_len = t // batch
    hd = HEAD_DIM
    nblk = n_heads // hp
    n_chunks = s_len // GDN_CHUNK
    assert n_heads % hp == 0 and s_len % (GDN_GROUP * GDN_CHUNK) == 0
    seq = lambda off: pl.BlockSpec((s_len, hp * hd), lambda b, h: (b, off + h))
    cw = lambda off: pl.BlockSpec((CONV_WIDTH, hp * hd), lambda b, h: (0, off + h))
    row = pl.BlockSpec((1, LANES), lambda b, h: (0, 0))
    act = pltpu.VMEM((s_len, LANES), F32)
    per_head = pltpu.VMEM((hp, s_len, hd), F32)
    return pl.pallas_call(
        functools.partial(_gdn_kernel, n_heads=n_heads, hp=hp),
        grid=(batch, nblk),
        in_specs=[seq(0), seq(nblk), seq(2 * nblk), seq(3 * nblk),
                  pl.BlockSpec((s_len, LANES), lambda b, h: (b, 0)),
                  cw(0), cw(nblk), cw(2 * nblk), row, row, row],
        out_specs=pl.BlockSpec((s_len, hp * hd), lambda b, h: (b, h)),
        out_shape=jax.ShapeDtypeStruct((t, n_heads * hd), BF16),
        scratch_shapes=[act, act, pltpu.VMEM((n_heads, 1, s_len), F32),
                        per_head, per_head, per_head, per_head, per_head,
                        pltpu.VMEM((hp, n_chunks, hd, hd), BF16), pltpu.VMEM((hp, n_chunks, hd, hd), F32),
                        pltpu.VMEM((hp, s_len, hd), BF16), per_head, pltpu.VMEM((CONV_PAD + s_len, hd), F32)],
        compiler_params=_cparams(2),
        name="gdn",
    )(proj, proj, proj, proj, gates, conv_w, conv_w, conv_w, alog_pad, dtb_pad, out_norm_w.reshape(1, hd))


def _gelu_tanh(x):
    return 0.5 * x * (1.0 + jnp.tanh(math.sqrt(2.0 / math.pi) * (x + 0.044715 * (x * x * x))))


def _lru_kernel(x_ref, y_ref, cw_ref, cb_ref, wa_ref, ba_ref, wx_ref, bx_ref, lam_ref, o_ref, pad_s):
    pad_s[:CONV_PAD, :] = jnp.zeros((CONV_PAD, pad_s.shape[1]), F32)
    xc = _causal_conv(x_ref[...].astype(F32), cw_ref[...], pad_s) + cb_ref[...]
    x16 = xc.astype(BF16)
    r = _sigmoid(_dot(x16, wa_ref[0]) + ba_ref[...])
    i = _sigmoid(_dot(x16, wx_ref[0]) + bx_ref[...])
    log_a = (-LRU_C) * r * _softplus(-lam_ref[...])
    a = jnp.exp(log_a)
    th = jnp.tanh(log_a)
    b = jnp.sqrt(-2.0 * th / (1.0 - th)) * i * xc
    row = lax.broadcasted_iota(jnp.int32, a.shape, 0)
    s_len, width = a.shape
    sublanes = 8

    def shifted(x, sh, fill):
        if sh % sublanes == 0:
            return jnp.concatenate([jnp.full((sh, width), fill, F32), x[:s_len - sh, :]], axis=0)
        return jnp.where(row >= sh, pltpu.roll(x, sh, 0), fill)

    sh = 1
    while sh < s_len:
        b = a * shifted(b, sh, 0.0) + b
        if sh * 2 < s_len:
            a = a * shifted(a, sh, 1.0)
        sh *= 2
    o_ref[...] = (b * _gelu_tanh(y_ref[...].astype(F32))).astype(o_ref.dtype)


def _lru(proj, conv_w, conv_b, wa, ba, wx, bx, lam, *, batch, col_off):
    t = proj.shape[0]
    s_len = t // batch
    n_blocks, blk = wa.shape[0], wa.shape[1]
    seq = lambda off: pl.BlockSpec((s_len, blk), lambda b, n: (b, off + n))
    vec = pl.BlockSpec((1, blk), lambda b, n: (0, n))
    mat = pl.BlockSpec((1, blk, blk), lambda b, n: (n, 0, 0))
    width = n_blocks * blk
    return pl.pallas_call(
        _lru_kernel,
        grid=(batch, n_blocks),
        in_specs=[seq(col_off), seq(col_off + n_blocks),
                  pl.BlockSpec((CONV_WIDTH, blk), lambda b, n: (0, n)), vec, mat, vec, mat, vec, vec],
        out_specs=pl.BlockSpec((s_len, blk), lambda b, n: (b, n)),
        out_shape=jax.ShapeDtypeStruct((t, width), BF16),
        scratch_shapes=[pltpu.VMEM((CONV_PAD + s_len, blk), F32)],
        compiler_params=_cparams(2),
        name="rglru",
    )(proj, proj, conv_w, conv_b.reshape(1, width), wa, ba.reshape(1, width), wx, bx.reshape(1, width),
      lam.reshape(1, width))


LOG2E = math.log2(math.e)


def _softmax2_pv(logits2, v16):
    mx = jnp.max(logits2, axis=-1, keepdims=True)
    p = jnp.exp2(logits2 - mx)
    denom = jnp.sum(p, axis=-1, keepdims=True)
    return _dot(p.astype(BF16), v16) / denom


def _causal_tile(logits2, lo, extra=None):
    tq = logits2.shape[0]
    tri = lax.broadcasted_iota(jnp.int32, (tq, tq), 1) <= lax.broadcasted_iota(jnp.int32, (tq, tq), 0)
    parts = [] if lo == 0 else ([logits2[:, :lo]] if extra is None else extra)
    parts.append(jnp.where(tri, logits2[:, lo:], -jnp.inf))
    return parts[0] if len(parts) == 1 else jnp.concatenate(parts, axis=1)


def _fox_kernel(q_ref, k_ref, v_ref, g_ref, fb_ref, o_ref, cfc_s, cfr_s, *, n_heads):
    h = pl.program_id(1)
    s_len = q_ref.shape[0]
    tq = ATTN_Q_TILE

    @pl.when(h == 0)
    def _():
        z = g_ref[...] + fb_ref[...]
        log_f = -_softplus(-z)
        rows = _lane_cumsum(log_f.T[0:n_heads, :], s_len)
        for r in range(n_heads):
            cfr_s[r] = rows[r:r + 1, :]
        cfc_s[...] = _rows_to_columns(rows, LANES)

    cf_col2 = _lane_column(cfc_s[...], h) * LOG2E
    cf_row2 = cfr_s[h] * LOG2E
    k16 = k_ref[...]
    v16 = v_ref[...]
    scale2 = (HEAD_DIM ** -0.5) * LOG2E
    n_tiles = s_len // tq
    qk = lambda i: _dot_nt(q_ref[i * tq:(i + 1) * tq, :], k16[:(i + 1) * tq, :])
    qk_next = qk(0)
    for i in range(n_tiles):
        lo, hi = i * tq, (i + 1) * tq
        qk_cur, qk_next = qk_next, (qk(i + 1) if i + 1 < n_tiles else None)
        logits2 = qk_cur * scale2 + cf_col2[lo:hi, :] - cf_row2[:, :hi]
        o_ref[lo:hi, :] = _softmax2_pv(_causal_tile(logits2, lo), v16[:hi, :]).astype(o_ref.dtype)


def _fox(proj, gates, fb_pad, *, batch, n_heads):
    t = proj.shape[0]
    s_len = t // batch
    hd = HEAD_DIM
    seq = lambda off: pl.BlockSpec((s_len, hd), lambda b, h: (b, off + h))
    return pl.pallas_call(
        functools.partial(_fox_kernel, n_heads=n_heads),
        grid=(batch, n_heads),
        in_specs=[seq(0), seq(n_heads), seq(2 * n_heads),
                  pl.BlockSpec((s_len, LANES), lambda b, h: (b, 0)),
                  pl.BlockSpec((1, LANES), lambda b, h: (0, 0))],
        out_specs=pl.BlockSpec((s_len, hd), lambda b, h: (b, h)),
        out_shape=jax.ShapeDtypeStruct((t, n_heads * hd), BF16),
        scratch_shapes=[pltpu.VMEM((s_len, LANES), F32), pltpu.VMEM((n_heads, 1, s_len), F32)],
        compiler_params=_cparams(2),
        name="fox",
    )(proj, proj, proj, gates, fb_pad)


def _rotary(x, cos_t, sin_lo, sin_hi):
    half = ROPE_DIM // 2
    return x * cos_t + pltpu.roll(x, LANES - half, 1) * sin_lo + pltpu.roll(x, half, 1) * sin_hi


def _moba_kernel(q_ref, k_ref, v_ref, cos_ref, slo_ref, shi_ref, o_ref, q_s, sel_s):
    s_len = q_ref.shape[0]
    blk = MOBA_BLOCK
    nb = s_len // blk
    cos_t, sin_lo, sin_hi = cos_ref[...], slo_ref[...], shi_ref[...]
    q = _rotary(q_ref[...].astype(F32), cos_t, sin_lo, sin_hi)
    k = _rotary(k_ref[...].astype(F32), cos_t, sin_lo, sin_hi)
    q_s[...] = q.astype(BF16)
    k16 = k.astype(BF16)
    v16 = v_ref[...]

    nb_pad = -(-nb // 8) * 8
    k_mean = jnp.concatenate([jnp.mean(k[n * blk:(n + 1) * blk, :], axis=0, keepdims=True) for n in range(nb)]
                             + ([jnp.zeros((nb_pad - nb, HEAD_DIM), F32)] if nb_pad > nb else []), axis=0)
    gate = lax.dot_general(k_mean, q, (((1,), (1,)), ((), ())), preferred_element_type=F32,
                           precision=lax.Precision.HIGHEST)
    blk_row = lax.broadcasted_iota(jnp.int32, gate.shape, 0)
    q_blk = lax.broadcasted_iota(jnp.int32, gate.shape, 1) // blk
    past = blk_row < q_blk
    gate = jnp.where(past, gate, -jnp.inf)
    rank = jnp.zeros(gate.shape, F32)
    for m in range(nb - 1):
        g_m = gate[m:m + 1, :]
        ahead = (g_m > gate) | ((g_m == gate) & (blk_row > m))
        rank = rank + jnp.where(ahead & (q_blk > m), 1.0, 0.0)
    keep_bias = jnp.where(past & (rank < MOBA_TOPK), 0.0, -jnp.inf)
    sel_s[...] = _rows_to_columns(keep_bias, LANES)

    scale2 = (HEAD_DIM ** -0.5) * LOG2E
    for i in range(nb):
        lo, hi = i * blk, (i + 1) * blk
        logits2 = _dot_nt(q_s[lo:hi, :], k16[:hi, :]) * scale2
        sel = sel_s[lo:hi, :]
        extra = [logits2[:, j * blk:(j + 1) * blk] + sel[:, j:j + 1] for j in range(i)]
        o_ref[lo:hi, :] = _softmax2_pv(_causal_tile(logits2, lo, extra), v16[:hi, :]).astype(o_ref.dtype)


def _rope_tables(s_len):
    half = ROPE_DIM // 2
    inv_freq = ROPE_THETA ** (-jnp.arange(half, dtype=F32) / half)
    ang = jnp.arange(s_len, dtype=F32)[:, None] * inv_freq[None, :]
    cos, sin = jnp.cos(ang), jnp.sin(ang)
    zeros = jnp.zeros((s_len, HEAD_DIM - ROPE_DIM), F32)
    z_half = jnp.zeros((s_len, half), F32)
    cos_t = jnp.concatenate([cos, cos, zeros + 1.0], axis=1)
    sin_lo = jnp.concatenate([-sin, z_half, zeros], axis=1)
    sin_hi = jnp.concatenate([z_half, sin, zeros], axis=1)
    return cos_t, sin_lo, sin_hi


def _moba(proj, *, batch, n_heads, col_off):
    t = proj.shape[0]
    s_len = t // batch
    hd = HEAD_DIM
    seq = lambda off: pl.BlockSpec((s_len, hd), lambda b, h: (b, off + h))
    tab = pl.BlockSpec((s_len, hd), lambda b, h: (0, 0))
    return pl.pallas_call(
        _moba_kernel,
        grid=(batch, n_heads),
        in_specs=[seq(col_off), seq(col_off + n_heads), seq(col_off + 2 * n_heads), tab, tab, tab],
        out_specs=pl.BlockSpec((s_len, hd), lambda b, h: (b, h)),
        out_shape=jax.ShapeDtypeStruct((t, n_heads * hd), BF16),
        scratch_shapes=[pltpu.VMEM((s_len, hd), BF16), pltpu.VMEM((s_len, LANES), F32)],
        compiler_params=_cparams(2),
        name="moba",
    )(proj, proj, proj, *_rope_tables(s_len))


def _pad_lanes(parts):
    row = jnp.concatenate([p.reshape(1, -1).astype(F32) for p in parts], axis=1)
    return jnp.pad(row, ((0, 0), (0, LANES - row.shape[1])))


def _regroup_kernel(w_ref, main_ref, gate_ref, *, main_runs, gate_runs):
    w = w_ref[...]
    main_ref[...] = jnp.concatenate([w[:, a:b] for a, b in main_runs], axis=1).astype(main_ref.dtype)
    gate = [w[:, a:b] for a, b in gate_runs]
    n_gate = sum(b - a for a, b in gate_runs)
    gate.append(jnp.zeros((w.shape[0], gate_ref.shape[1] - n_gate), w.dtype))
    gate_ref[...] = jnp.concatenate(gate, axis=1).astype(gate_ref.dtype)


def _split_w_in(w, layer, sizes, main_idx, gate_idx, *, tr=256):
    offs = [0]
    for n in sizes:
        offs.append(offs[-1] + n)

    def runs(idx):
        out = []
        for i in idx:
            if out and out[-1][1] == offs[i]:
                out[-1] = (out[-1][0], offs[i + 1])
            else:
                out.append((offs[i], offs[i + 1]))
        return tuple(out)

    _, d, n_in = w.shape
    n_main = sum(sizes[i] for i in main_idx)
    return pl.pallas_call(
        functools.partial(_regroup_kernel, main_runs=runs(main_idx), gate_runs=runs(gate_idx)),
        grid=(d // tr,),
        in_specs=[pl.BlockSpec((None, tr, n_in), lambda i: (layer, i, 0))],
        out_specs=[pl.BlockSpec((tr, n_main), lambda i: (i, 0)), pl.BlockSpec((tr, LANES), lambda i: (i, 0))],
        out_shape=[jax.ShapeDtypeStruct((d, n_main), BF16), jax.ShapeDtypeStruct((d, LANES), BF16)],
        compiler_params=_cparams(1),
        name="regroup_w_in",
    )(w)


def kernel(x, ab_norm, ab_w_in, ab_conv_qkv, ab_a_log, ab_dt_bias, ab_out_norm, ab_lru_conv_w, ab_lru_conv_b,
           ab_lru_wa, ab_lru_ba, ab_lru_wx, ab_lru_bx, ab_lru_lambda, ab_w_out, cd_norm, cd_w_in, cd_f_bias,
           cd_w_out, ffn_norm, ffn_w_gate, ffn_w_up, ffn_w_down, final_norm):
    batch, s_len, d_model = x.shape
    depth = ffn_norm.shape[0]
    gdn_heads = ab_a_log.shape[1]
    gdn_width = gdn_heads * HEAD_DIM
    lru_width = ab_lru_lambda.shape[1]
    fox_heads = cd_f_bias.shape[1]
    fox_width = fox_heads * HEAD_DIM
    moba_width = d_model - fox_width
    moba_heads = moba_width // HEAD_DIM
    ab_sizes = (gdn_width,) * 4 + (gdn_heads,) * 2 + (lru_width,) * 2
    cd_sizes = (fox_width,) * 3 + (fox_heads,) + (moba_width,) * 3

    w_gate16, w_up16, w_down16 = (w.astype(BF16) for w in (ffn_w_gate, ffn_w_up, ffn_w_down))
    h = x.reshape(batch * s_len, d_model)
    for layer in range(depth):
        j = layer // 2
        if layer % 2 == 0:
            w_main, w_gate = _split_w_in(ab_w_in, j, ab_sizes, (0, 1, 2, 3, 6, 7), (4, 5))
            proj, gates = _inproj(h, ab_norm[j], w_main, w_gate)
            zeros_h = jnp.zeros((gdn_heads,), F32)
            o_a = _gdn(proj, gates, ab_conv_qkv[j], _pad_lanes([zeros_h, ab_a_log[j]]),
                       _pad_lanes([zeros_h, ab_dt_bias[j]]), ab_out_norm[j], batch=batch, n_heads=gdn_heads)
            o_b = _lru(proj, ab_lru_conv_w[j], ab_lru_conv_b[j], ab_lru_wa[j].astype(BF16), ab_lru_ba[j],
                       ab_lru_wx[j].astype(BF16), ab_lru_bx[j], ab_lru_lambda[j], batch=batch,
                       col_off=4 * gdn_heads)
            h = _outproj(h, o_a, o_b, ab_w_out[j].astype(BF16))
        else:
            w_main, w_gate = _split_w_in(cd_w_in, j, cd_sizes, (0, 1, 2, 4, 5, 6), (3,))
            proj, gates = _inproj(h, cd_norm[j], w_main, w_gate)
            o_c = _fox(proj, gates, _pad_lanes([cd_f_bias[j]]), batch=batch, n_heads=fox_heads)
            o_d = _moba(proj, batch=batch, n_heads=moba_heads, col_off=3 * fox_heads)
            h = _outproj(h, o_c, o_d, cd_w_out[j].astype(BF16))
        h = _ffn(h, ffn_norm[layer], w_gate16, w_up16, w_down16, final_norm, layer=layer,
                 final_norm=(layer == depth - 1))
    return h.reshape(batch, s_len, d_model)
```

```python
import functools
import math

import jax
import jax.numpy as jnp
from jax import lax
from jax.experimental import pallas as pl
from jax.experimental.pallas import tpu as pltpu

F32 = jnp.float32
BF16 = jnp.bfloat16

HEAD_DIM = 128
LANES = 128
NORM_EPS = 1e-6
CONV_WIDTH = 4
GDN_CHUNK = 64
GDN_GROUP = 8
GDN_HEADS_PER_STEP = 2
LRU_C = 8.0
MOBA_BLOCK = 256
MOBA_TOPK = 3
ROPE_THETA = 500000.0
ROPE_DIM = HEAD_DIM // 4
ATTN_Q_TILE = 256
VMEM_LIMIT_BYTES = 48 * 1024 * 1024
FFN_VMEM_LIMIT_BYTES = 58 * 1024 * 1024


def _cparams(n_axes, vmem_limit_bytes=VMEM_LIMIT_BYTES):
    return pltpu.CompilerParams(dimension_semantics=("arbitrary",) * n_axes,
                                vmem_limit_bytes=vmem_limit_bytes)


def _dot(a, b):
    return jnp.dot(a, b, preferred_element_type=F32)


def _dot_nt(a, b):
    return lax.dot_general(a, b, (((1,), (1,)), ((), ())), preferred_element_type=F32)


def _dot_tn(a, b):
    return lax.dot_general(a, b, (((0,), (0,)), ((), ())), preferred_element_type=F32)


def _rmsnorm(x, w):
    ms = jnp.mean(x * x, axis=-1, keepdims=True)
    return x * lax.rsqrt(ms + NORM_EPS) * w


def _sigmoid(x):
    return 0.5 * jnp.tanh(0.5 * x) + 0.5


def _softplus(x):
    return jnp.maximum(x, 0.0) + jnp.log1p(jnp.exp(-jnp.abs(x)))


def _silu(x):
    half = 0.5 * x
    return half + half * jnp.tanh(half)


CONV_PAD = 8


def _causal_conv(x, w, pad_ref):
    s_len = x.shape[0]
    k = w.shape[0]
    pad_ref[CONV_PAD:, :] = x
    acc = x * w[k - 1:k, :]
    for j in range(k - 1):
        acc = acc + pad_ref[CONV_PAD - (k - 1 - j):CONV_PAD - (k - 1 - j) + s_len, :] * w[j:j + 1, :]
    return acc


def _lane_column(x, idx):
    lane = lax.broadcasted_iota(jnp.int32, x.shape, 1)
    return jnp.sum(jnp.where(lane == idx, x, 0.0), axis=-1, keepdims=True)


def _lane_cumsum(x, period):
    pos = lax.broadcasted_iota(jnp.int32, x.shape, 1) % period
    sh = 1
    while sh < period:
        x = x + jnp.where(pos >= sh, pltpu.roll(x, sh, 1), 0.0)
        sh *= 2
    return x


def _rows_to_columns(rows8, n_lanes):
    pad = jnp.zeros((n_lanes - rows8.shape[0], rows8.shape[1]), F32)
    return jnp.concatenate([rows8, pad], axis=0).T


def _inproj_kernel(x_ref, nw_ref, w_ref, wg_ref, o_ref, og_ref, xn_ref):
    @pl.when(pl.program_id(1) == 0)
    def _():
        xn = _rmsnorm(x_ref[...], nw_ref[...]).astype(BF16)
        xn_ref[...] = xn
        og_ref[...] = _dot(xn, wg_ref[...])

    o_ref[...] = _dot(xn_ref[...], w_ref[...]).astype(o_ref.dtype)


def _inproj(x, nw, w, wg, *, tm=1024, tn=1024):
    t, d = x.shape
    n = w.shape[1]
    return pl.pallas_call(
        _inproj_kernel,
        grid=(t // tm, n // tn),
        in_specs=[pl.BlockSpec((tm, d), lambda i, j: (i, 0)),
                  pl.BlockSpec((1, d), lambda i, j: (0, 0)),
                  pl.BlockSpec((d, tn), lambda i, j: (0, j)),
                  pl.BlockSpec((d, LANES), lambda i, j: (0, 0))],
        out_specs=[pl.BlockSpec((tm, tn), lambda i, j: (i, j)),
                   pl.BlockSpec((tm, LANES), lambda i, j: (i, 0))],
        out_shape=[jax.ShapeDtypeStruct((t, n), BF16), jax.ShapeDtypeStruct((t, LANES), F32)],
        scratch_shapes=[pltpu.VMEM((tm, d), BF16)],
        compiler_params=_cparams(2),
        name="inproj",
    )(x, nw.reshape(1, d), w, wg)


def _outproj_kernel(res_ref, a1_ref, a2_ref, w1_ref, w2_ref, o_ref):
    o_ref[...] = res_ref[...] + _dot(a1_ref[...], w1_ref[...]) + _dot(a2_ref[...], w2_ref[...])


def _outproj(res, a1, a2, w, *, tm=512, tn=2048):
    t, d = res.shape
    k1, k2 = a1.shape[1], a2.shape[1]
    assert k1 == k2 and w.shape == (k1 + k2, d)
    return pl.pallas_call(
        _outproj_kernel,
        grid=(t // tm, d // tn),
        in_specs=[pl.BlockSpec((tm, tn), lambda i, j: (i, j)),
                  pl.BlockSpec((tm, k1), lambda i, j: (i, 0)),
                  pl.BlockSpec((tm, k2), lambda i, j: (i, 0)),
                  pl.BlockSpec((k1, tn), lambda i, j: (0, j)),
                  pl.BlockSpec((k2, tn), lambda i, j: (1, j))],
        out_specs=pl.BlockSpec((tm, tn), lambda i, j: (i, j)),
        out_shape=jax.ShapeDtypeStruct((t, d), F32),
        compiler_params=_cparams(2),
        name="outproj",
    )(res, a1, a2, w, w)


def _ffn_kernel(x_ref, nw_ref, wg_ref, wu_ref, wd_ref, fnw_ref, o_ref, xn_ref, *, final_norm, down_chunk):
    f = pl.program_id(1)

    @pl.when(f == 0)
    def _():
        x = x_ref[...]
        xn_ref[...] = _rmsnorm(x, nw_ref[...]).astype(BF16)
        o_ref[...] = x

    xn = xn_ref[...]
    act = (_silu(_dot(xn, wg_ref[...])) * _dot(xn, wu_ref[...])).astype(BF16)
    for c0 in range(0, o_ref.shape[1], down_chunk):
        o_ref[:, c0:c0 + down_chunk] += _dot(act, wd_ref[:, c0:c0 + down_chunk])

    if final_norm:
        @pl.when(f == pl.num_programs(1) - 1)
        def _():
            o_ref[...] = _rmsnorm(o_ref[...], fnw_ref[...])


def _ffn(x, nw, wg, wu, wd, fnw, *, layer, final_norm, tm=1024, tf=512, down_chunk=512):
    t, d = x.shape
    dff = wg.shape[2]
    return pl.pallas_call(
        functools.partial(_ffn_kernel, final_norm=final_norm, down_chunk=down_chunk),
        grid=(t // tm, dff // tf),
        in_specs=[pl.BlockSpec((tm, d), lambda i, f: (i, 0)),
                  pl.BlockSpec((1, d), lambda i, f: (0, 0)),
                  pl.BlockSpec((None, d, tf), lambda i, f: (layer, 0, f)),
                  pl.BlockSpec((None, d, tf), lambda i, f: (layer, 0, f)),
                  pl.BlockSpec((None, tf, d), lambda i, f: (layer, f, 0)),
                  pl.BlockSpec((1, d), lambda i, f: (0, 0))],
        out_specs=pl.BlockSpec((tm, d), lambda i, f: (i, 0)),
        out_shape=jax.ShapeDtypeStruct((t, d), F32),
        scratch_shapes=[pltpu.VMEM((tm, d), BF16)],
        compiler_params=_cparams(2, FFN_VMEM_LIMIT_BYTES),
        name="ffn",
    )(x, nw.reshape(1, d), wg, wu, wd, fnw.reshape(1, d))


def _unit_lower_inverses(ms):
    c = ms[0].shape[0]
    eye = (lax.broadcasted_iota(jnp.int32, (c, c), 0) == lax.broadcasted_iota(jnp.int32, (c, c), 1)).astype(F32)
    invs = [eye - m for m in ms]
    m16s = [m.astype(BF16) for m in ms]
    ps = [_dot(m16, m16) for m16 in m16s]
    yield
    span = 2
    while span < c:
        p16s = [p.astype(BF16) for p in ps]
        if span * 2 < c:
            boths = [_dot(jnp.concatenate([inv.astype(BF16), p16], axis=0), p16) for inv, p16 in zip(invs, p16s)]
            invs = [inv + both[:c] for inv, both in zip(invs, boths)]
            ps = [both[c:] for both in boths]
        else:
            invs = [inv + _dot(inv.astype(BF16), p16) for inv, p16 in zip(invs, p16s)]
        span *= 2
        yield
    return invs


def _gdn_kernel(q_ref, k_ref, v_ref, z_ref, g_ref, cq_ref, ck_ref, cv_ref, alog_ref, dtb_ref, onw_ref,
                o_ref, beta_s, gcc_s, gcr_s, q_s, k_s, v_s, bc_s, gc_s, p_s, b_s, qp_s, out_s, pad_s,
                *, n_heads, hp):
    hb = pl.program_id(1)
    s_len = q_ref.shape[0]
    c = GDN_CHUNK
    group = GDN_GROUP
    hd = HEAD_DIM

    @pl.when(hb == 0)
    def _():
        logits = g_ref[...]
        beta_s[...] = _sigmoid(logits)
        log_decay = -jnp.exp(alog_ref[...]) * _softplus(logits + dtb_ref[...])
        rows = log_decay.T[n_heads:2 * n_heads, :]
        rows = _lane_cumsum(rows, c)
        for r in range(n_heads):
            gcr_s[r] = rows[r:r + 1, :]
        gcc_s[...] = _rows_to_columns(rows, LANES)

    pad_s[:CONV_PAD, :] = jnp.zeros((CONV_PAD, hd), F32)
    for j in range(hp):
        cols = slice(j * hd, (j + 1) * hd)
        head = hb * hp + j
        bc_s[j] = jnp.broadcast_to(_lane_column(beta_s[...], head), (s_len, hd))
        gc_s[j] = jnp.broadcast_to(_lane_column(gcc_s[...], head), (s_len, hd))
        q = _silu(_causal_conv(q_ref[:, cols].astype(F32), cq_ref[:, cols], pad_s))
        k = _silu(_causal_conv(k_ref[:, cols].astype(F32), ck_ref[:, cols], pad_s))
        v_s[j] = _silu(_causal_conv(v_ref[:, cols].astype(F32), cv_ref[:, cols], pad_s))
        q_s[j] = q * (lax.rsqrt(jnp.sum(q * q, axis=-1, keepdims=True) + NORM_EPS) * (hd ** -0.5))
        k_s[j] = k * lax.rsqrt(jnp.sum(k * k, axis=-1, keepdims=True) + NORM_EPS)

    ri = lax.broadcasted_iota(jnp.int32, (c, c), 0)
    ci = lax.broadcasted_iota(jnp.int32, (c, c), 1)
    incl = ri >= ci
    strict = ri > ci

    def chunk_terms(gi):
        lanes = pl.ds(pl.multiple_of(gi * (group * c), group * c), group * c)
        gc_rows = [gcr_s[hb * hp + j, :, lanes] for j in range(hp)]
        units = [(j, g) for g in range(group) for j in range(hp)]
        rows_u, scores_u, decay_u, rhs_u, kdec_u, qdec_u = [], [], [], [], [], []
        for j, g in units:
            rows = pl.ds(pl.multiple_of((gi * group + g) * c, c), c)
            qc, kc, vc = q_s[j, rows, :], k_s[j, rows, :], v_s[j, rows, :]
            beta, gc = bc_s[j, rows, :], gc_s[j, rows, :]
            gc_row = gc_rows[j][:, g * c:(g + 1) * c]
            decay_u.append(jnp.where(incl, jnp.exp(jnp.where(incl, gc[:, :c] - gc_row, 0.0)), 0.0))
            k_beta = kc * beta
            e_gc = jnp.exp(gc)
            lhs = jnp.concatenate([k_beta.astype(BF16), qc.astype(BF16)], axis=0)
            scores_u.append(_dot_nt(lhs, kc.astype(BF16)))
            rhs_u.append(jnp.concatenate([(vc * beta).astype(BF16), (k_beta * e_gc).astype(BF16)], axis=1))
            kdec_u.append((kc * jnp.exp(gc[c - 1:c, :] - gc)).astype(BF16))
            qdec_u.append(qc * e_gc)
            rows_u.append(rows)
        yield
        ms = [jnp.where(strict, sc[:c] * dec, 0.0) for sc, dec in zip(scores_u, decay_u)]
        attn_u = [jnp.where(incl, sc[c:] * dec, 0.0).astype(BF16) for sc, dec in zip(scores_u, decay_u)]
        t_u = yield from _unit_lower_inverses(ms)
        uw_u = [_dot(t.astype(BF16), rhs).astype(BF16) for t, rhs in zip(t_u, rhs_u)]
        yield
        bp_u = [_dot_tn(kdec, uw) for kdec, uw in zip(kdec_u, uw_u)]
        yield
        aw_u = [_dot(attn, uw) for attn, uw in zip(attn_u, uw_u)]
        for (j, g), rows, bp, aw, qdec in zip(units, rows_u, bp_u, aw_u, qdec_u):
            n = gi * group + g
            b_s[j, n] = bp[:, :hd]
            p_s[j, n] = bp[:, hd:].astype(BF16)
            qp_s[j, rows, :] = (qdec - aw[:, hd:]).astype(BF16)
            out_s[j, rows, :] = aw[:, :hd]
        yield

    def recurrence_steps(gi, states):
        def step(n):
            r0 = pl.multiple_of(n * c, c)
            rows = pl.ds(r0, c)
            for j in range(hp):
                s16 = states[j].astype(BF16)
                out_s[j, rows, :] += _dot(qp_s[j, rows, :], s16)
                gc_tail = gc_s[j, pl.ds(pl.multiple_of(r0 + c - 8, 8), 8), :]
                states[j] = states[j] * jnp.exp(gc_tail[7:8, :]) - _dot(p_s[j, n], s16) + b_s[j, n]
        return [functools.partial(step, gi * group + g) for g in range(group)]

    def interleave(stages, steps):
        steps = list(steps)
        for _ in stages:
            if steps:
                steps.pop(0)()
        for step in steps:
            step()

    def group_body(gi, states):
        states = list(states)
        interleave(chunk_terms(gi), recurrence_steps(gi - 1, states))
        return tuple(states)

    n_groups = s_len // (group * c)
    interleave(chunk_terms(0), [])
    states = lax.fori_loop(1, n_groups, group_body, tuple(jnp.zeros((hd, hd), F32) for _ in range(hp)))
    states = list(states)
    interleave(iter(()), recurrence_steps(n_groups - 1, states))

    for j in range(hp):
        cols = slice(j * hd, (j + 1) * hd)
        o = _rmsnorm(out_s[j], onw_ref[...]) * _silu(z_ref[:, cols].astype(F32))
        o_ref[:, cols] = o.astype(o_ref.dtype)


def _gdn(proj, gates, conv_w, alog_pad, dtb_pad, out_norm_w, *, batch, n_heads, hp=GDN_HEADS_PER_STEP):
    t = proj.shape[0]
    s_len = t // batch
    hd = HEAD_DIM
    nblk = n_heads // hp
    n_chunks = s_len // GDN_CHUNK
    assert n_heads % hp == 0 and s_len % (GDN_GROUP * GDN_CHUNK) == 0
    seq = lambda off: pl.BlockSpec((s_len, hp * hd), lambda b, h: (b, off + h))
    cw = lambda off: pl.BlockSpec((CONV_WIDTH, hp * hd), lambda b, h: (0, off + h))
    row = pl.BlockSpec((1, LANES), lambda b, h: (0, 0))
    act = pltpu.VMEM((s_len, LANES), F32)
    per_head = pltpu.VMEM((hp, s_len, hd), F32)
    return pl.pallas_call(
        functools.partial(_gdn_kernel, n_heads=n_heads, hp=hp),
        grid=(batch, nblk),
        in_specs=[seq(0), seq(nblk), seq(2 * nblk), seq(3 * nblk),
                  pl.BlockSpec((s_len, LANES), lambda b, h: (b, 0)),
                  cw(0), cw(nblk), cw(2 * nblk), row, row, row],
        out_specs=pl.BlockSpec((s_len, hp * hd), lambda b, h: (b, h)),
        out_shape=jax.ShapeDtypeStruct((t, n_heads * hd), BF16),
        scratch_shapes=[act, act, pltpu.VMEM((n_heads, 1, s_len), F32),
                        per_head, per_head, per_head, per_head, per_head,
                        pltpu.VMEM((hp, n_chunks, hd, hd), BF16), pltpu.VMEM((hp, n_chunks, hd, hd), F32),
                        pltpu.VMEM((hp, s_len, hd), BF16), per_head, pltpu.VMEM((CONV_PAD + s_len, hd), F32)],
        compiler_params=_cparams(2),
        name="gdn",
    )(proj, proj, proj, proj, gates, conv_w, conv_w, conv_w, alog_pad, dtb_pad, out_norm_w.reshape(1, hd))


def _gelu_tanh(x):
    return 0.5 * x * (1.0 + jnp.tanh(math.sqrt(2.0 / math.pi) * (x + 0.044715 * (x * x * x))))


def _lru_kernel(x_ref, y_ref, cw_ref, cb_ref, wa_ref, ba_ref, wx_ref, bx_ref, lam_ref, o_ref, pad_s):
    pad_s[:CONV_PAD, :] = jnp.zeros((CONV_PAD, pad_s.shape[1]), F32)
    xc = _causal_conv(x_ref[...].astype(F32), cw_ref[...], pad_s) + cb_ref[...]
    x16 = xc.astype(BF16)
    r = _sigmoid(_dot(x16, wa_ref[0]) + ba_ref[...])
    i = _sigmoid(_dot(x16, wx_ref[0]) + bx_ref[...])
    log_a = (-LRU_C) * r * _softplus(-lam_ref[...])
    a = jnp.exp(log_a)
    th = jnp.tanh(log_a)
    b = jnp.sqrt(-2.0 * th / (1.0 - th)) * i * xc
    row = lax.broadcasted_iota(jnp.int32, a.shape, 0)
    s_len, width = a.shape
    sublanes = 8

    def shifted(x, sh, fill):
        if sh % sublanes == 0:
            return jnp.concatenate([jnp.full((sh, width), fill, F32), x[:s_len - sh, :]], axis=0)
        return jnp.where(row >= sh, pltpu.roll(x, sh, 0), fill)

    sh = 1
    while sh < s_len:
        b = a * shifted(b, sh, 0.0) + b
        if sh * 2 < s_len:
            a = a * shifted(a, sh, 1.0)
        sh *= 2
    o_ref[...] = (b * _gelu_tanh(y_ref[...].astype(F32))).astype(o_ref.dtype)


def _lru(proj, conv_w, conv_b, wa, ba, wx, bx, lam, *, batch, col_off):
    t = proj.shape[0]
    s_len = t // batch
    n_blocks, blk = wa.shape[0], wa.shape[1]
    seq = lambda off: pl.BlockSpec((s_len, blk), lambda b, n: (b, off + n))
    vec = pl.BlockSpec((1, blk), lambda b, n: (0, n))
    mat = pl.BlockSpec((1, blk, blk), lambda b, n: (n, 0, 0))
    width = n_blocks * blk
    return pl.pallas_call(
        _lru_kernel,
        grid=(batch, n_blocks),
        in_specs=[seq(col_off), seq(col_off + n_blocks),
                  pl.BlockSpec((CONV_WIDTH, blk), lambda b, n: (0, n)), vec, mat, vec, mat, vec, vec],
        out_specs=pl.BlockSpec((s_len, blk), lambda b, n: (b, n)),
        out_shape=jax.ShapeDtypeStruct((t, width), BF16),
        scratch_shapes=[pltpu.VMEM((CONV_PAD + s_len, blk), F32)],
        compiler_params=_cparams(2),
        name="rglru",
    )(proj, proj, conv_w, conv_b.reshape(1, width), wa, ba.reshape(1, width), wx, bx.reshape(1, width),
      lam.reshape(1, width))


LOG2E = math.log2(math.e)


def _softmax2_pv(logits2, v16):
    mx = jnp.max(logits2, axis=-1, keepdims=True)
    p = jnp.exp2(logits2 - mx)
    denom = jnp.sum(p, axis=-1, keepdims=True)
    return _dot(p.astype(BF16), v16) / denom


def _causal_tile(logits2, lo, extra=None):
    tq = logits2.shape[0]
    tri = lax.broadcasted_iota(jnp.int32, (tq, tq), 1) <= lax.broadcasted_iota(jnp.int32, (tq, tq), 0)
    parts = [] if lo == 0 else ([logits2[:, :lo]] if extra is None else extra)
    parts.append(jnp.where(tri, logits2[:, lo:], -jnp.inf))
    return parts[0] if len(parts) == 1 else jnp.concatenate(parts, axis=1)


def _fox_kernel(q_ref, k_ref, v_ref, g_ref, fb_ref, o_ref, cfc_s, cfr_s, *, n_heads):
    h = pl.program_id(1)
    s_len = q_ref.shape[0]
    tq = ATTN_Q_TILE

    @pl.when(h == 0)
    def _():
        z = g_ref[...] + fb_ref[...]
        log_f = -_softplus(-z)
        rows = _lane_cumsum(log_f.T[0:n_heads, :], s_len)
        for r in range(n_heads):
            cfr_s[r] = rows[r:r + 1, :]
        cfc_s[...] = _rows_to_columns(rows, LANES)

    cf_col2 = _lane_column(cfc_s[...], h) * LOG2E
    cf_row2 = cfr_s[h] * LOG2E
    k16 = k_ref[...]
    v16 = v_ref[...]
    scale2 = (HEAD_DIM ** -0.5) * LOG2E
    n_tiles = s_len // tq
    qk = lambda i: _dot_nt(q_ref[i * tq:(i + 1) * tq, :], k16[:(i + 1) * tq, :])
    qk_next = qk(0)
    for i in range(n_tiles):
        lo, hi = i * tq, (i + 1) * tq
        qk_cur, qk_next = qk_next, (qk(i + 1) if i + 1 < n_tiles else None)
        logits2 = qk_cur * scale2 + cf_col2[lo:hi, :] - cf_row2[:, :hi]
        o_ref[lo:hi, :] = _softmax2_pv(_causal_tile(logits2, lo), v16[:hi, :]).astype(o_ref.dtype)


def _fox(proj, gates, fb_pad, *, batch, n_heads):
    t = proj.shape[0]
    s_len = t // batch
    hd = HEAD_DIM
    seq = lambda off: pl.BlockSpec((s_len, hd), lambda b, h: (b, off + h))
    return pl.pallas_call(
        functools.partial(_fox_kernel, n_heads=n_heads),
        grid=(batch, n_heads),
        in_specs=[seq(0), seq(n_heads), seq(2 * n_heads),
                  pl.BlockSpec((s_len, LANES), lambda b, h: (b, 0)),
                  pl.BlockSpec((1, LANES), lambda b, h: (0, 0))],
        out_specs=pl.BlockSpec((s_len, hd), lambda b, h: (b, h)),
        out_shape=jax.ShapeDtypeStruct((t, n_heads * hd), BF16),
        scratch_shapes=[pltpu.VMEM((s_len, LANES), F32), pltpu.VMEM((n_heads, 1, s_len), F32)],
        compiler_params=_cparams(2),
        name="fox",
    )(proj, proj, proj, gates, fb_pad)


def _rotary(x, cos_t, sin_lo, sin_hi):
    half = ROPE_DIM // 2
    return x * cos_t + pltpu.roll(x, LANES - half, 1) * sin_lo + pltpu.roll(x, half, 1) * sin_hi


def _moba_kernel(q_ref, k_ref, v_ref, cos_ref, slo_ref, shi_ref, o_ref, q_s, sel_s):
    s_len = q_ref.shape[0]
    blk = MOBA_BLOCK
    nb = s_len // blk
    cos_t, sin_lo, sin_hi = cos_ref[...], slo_ref[...], shi_ref[...]
    q = _rotary(q_ref[...].astype(F32), cos_t, sin_lo, sin_hi)
    k = _rotary(k_ref[...].astype(F32), cos_t, sin_lo, sin_hi)
    q_s[...] = q.astype(BF16)
    k16 = k.astype(BF16)
    v16 = v_ref[...]

    nb_pad = -(-nb // 8) * 8
    k_mean = jnp.concatenate([jnp.mean(k[n * blk:(n + 1) * blk, :], axis=0, keepdims=True) for n in range(nb)]
                             + ([jnp.zeros((nb_pad - nb, HEAD_DIM), F32)] if nb_pad > nb else []), axis=0)
    q_hi = q_s[...]
    q_lo = (q - q_hi.astype(F32)).astype(BF16)
    km_hi = k_mean.astype(BF16)
    km_lo = (k_mean - km_hi.astype(F32)).astype(BF16)
    gate = (_dot_nt(km_hi, q_hi) + _dot_nt(km_hi, q_lo)) + (_dot_nt(km_lo, q_hi) + _dot_nt(km_lo, q_lo))
    blk_row = lax.broadcasted_iota(jnp.int32, gate.shape, 0)
    q_blk = lax.broadcasted_iota(jnp.int32, gate.shape, 1) // blk
    past = blk_row < q_blk
    gate = jnp.where(past, gate, -jnp.inf)
    rank = jnp.zeros(gate.shape, F32)
    for m in range(nb - 1):
        g_m = gate[m:m + 1, :]
        ahead = (g_m > gate) | ((g_m == gate) & (blk_row > m))
        rank = rank + jnp.where(ahead & (q_blk > m), 1.0, 0.0)
    keep_bias = jnp.where(past & (rank < MOBA_TOPK), 0.0, -jnp.inf)
    sel_s[...] = _rows_to_columns(keep_bias, LANES)

    scale2 = (HEAD_DIM ** -0.5) * LOG2E
    for i in range(nb):
        lo, hi = i * blk, (i + 1) * blk
        logits2 = _dot_nt(q_s[lo:hi, :], k16[:hi, :]) * scale2
        sel = sel_s[lo:hi, :]
        extra = [logits2[:, j * blk:(j + 1) * blk] + sel[:, j:j + 1] for j in range(i)]
        o_ref[lo:hi, :] = _softmax2_pv(_causal_tile(logits2, lo, extra), v16[:hi, :]).astype(o_ref.dtype)


def _rope_tables(s_len):
    half = ROPE_DIM // 2
    inv_freq = ROPE_THETA ** (-jnp.arange(half, dtype=F32) / half)
    ang = jnp.arange(s_len, dtype=F32)[:, None] * inv_freq[None, :]
    cos, sin = jnp.cos(ang), jnp.sin(ang)
    zeros = jnp.zeros((s_len, HEAD_DIM - ROPE_DIM), F32)
    z_half = jnp.zeros((s_len, half), F32)
    cos_t = jnp.concatenate([cos, cos, zeros + 1.0], axis=1)
    sin_lo = jnp.concatenate([-sin, z_half, zeros], axis=1)
    sin_hi = jnp.concatenate([z_half, sin, zeros], axis=1)
    return cos_t, sin_lo, sin_hi


def _moba(proj, *, batch, n_heads, col_off):
    t = proj.shape[0]
    s_len = t // batch
    hd = HEAD_DIM
    seq = lambda off: pl.BlockSpec((s_len, hd), lambda b, h: (b, off + h))
    tab = pl.BlockSpec((s_len, hd), lambda b, h: (0, 0))
    return pl.pallas_call(
        _moba_kernel,
        grid=(batch, n_heads),
        in_specs=[seq(col_off), seq(col_off + n_heads), seq(col_off + 2 * n_heads), tab, tab, tab],
        out_specs=pl.BlockSpec((s_len, hd), lambda b, h: (b, h)),
        out_shape=jax.ShapeDtypeStruct((t, n_heads * hd), BF16),
        scratch_shapes=[pltpu.VMEM((s_len, hd), BF16), pltpu.VMEM((s_len, LANES), F32)],
        compiler_params=_cparams(2),
        name="moba",
    )(proj, proj, proj, *_rope_tables(s_len))


def _pad_lanes(parts):
    row = jnp.concatenate([p.reshape(1, -1).astype(F32) for p in parts], axis=1)
    return jnp.pad(row, ((0, 0), (0, LANES - row.shape[1])))


def _regroup_kernel(w_ref, main_ref, gate_ref, *, main_runs, gate_runs):
    w = w_ref[...]
    main_ref[...] = jnp.concatenate([w[:, a:b] for a, b in main_runs], axis=1).astype(main_ref.dtype)
    gate = [w[:, a:b] for a, b in gate_runs]
    n_gate = sum(b - a for a, b in gate_runs)
    gate.append(jnp.zeros((w.shape[0], gate_ref.shape[1] - n_gate), w.dtype))
    gate_ref[...] = jnp.concatenate(gate, axis=1).astype(gate_ref.dtype)


def _split_w_in(w, layer, sizes, main_idx, gate_idx, *, tr=256):
    offs = [0]
    for n in sizes:
        offs.append(offs[-1] + n)

    def runs(idx):
        out = []
        for i in idx:
            if out and out[-1][1] == offs[i]:
                out[-1] = (out[-1][0], offs[i + 1])
            else:
                out.append((offs[i], offs[i + 1]))
        return tuple(out)

    _, d, n_in = w.shape
    n_main = sum(sizes[i] for i in main_idx)
    return pl.pallas_call(
        functools.partial(_regroup_kernel, main_runs=runs(main_idx), gate_runs=runs(gate_idx)),
        grid=(d // tr,),
        in_specs=[pl.BlockSpec((None, tr, n_in), lambda i: (layer, i, 0))],
        out_specs=[pl.BlockSpec((tr, n_main), lambda i: (i, 0)), pl.BlockSpec((tr, LANES), lambda i: (i, 0))],
        out_shape=[jax.ShapeDtypeStruct((d, n_main), BF16), jax.ShapeDtypeStruct((d, LANES), BF16)],
        compiler_params=_cparams(1),
        name="regroup_w_in",
    )(w)


def kernel(x, ab_norm, ab_w_in, ab_conv_qkv, ab_a_log, ab_dt_bias, ab_out_norm, ab_lru_conv_w, ab_lru_conv_b,
           ab_lru_wa, ab_lru_ba, ab_lru_wx, ab_lru_bx, ab_lru_lambda, ab_w_out, cd_norm, cd_w_in, cd_f_bias,
           cd_w_out, ffn_norm, ffn_w_gate, ffn_w_up, ffn_w_down, final_norm):
    batch, s_len, d_model = x.shape
    depth = ffn_norm.shape[0]
    gdn_heads = ab_a_log.shape[1]
    gdn_width = gdn_heads * HEAD_DIM
    lru_width = ab_lru_lambda.shape[1]
    fox_heads = cd_f_bias.shape[1]
    fox_width = fox_heads * HEAD_DIM
    moba_width = d_model - fox_width
    moba_heads = moba_width // HEAD_DIM
    ab_sizes = (gdn_width,) * 4 + (gdn_heads,) * 2 + (lru_width,) * 2
    cd_sizes = (fox_width,) * 3 + (fox_heads,) + (moba_width,) * 3

    w_gate16, w_up16, w_down16 = (w.astype(BF16) for w in (ffn_w_gate, ffn_w_up, ffn_w_down))
    h = x.reshape(batch * s_len, d_model)
    for layer in range(depth):
        j = layer // 2
        if layer % 2 == 0:
            w_main, w_gate = _split_w_in(ab_w_in, j, ab_sizes, (0, 1, 2, 3, 6, 7), (4, 5))
            proj, gates = _inproj(h, ab_norm[j], w_main, w_gate)
            zeros_h = jnp.zeros((gdn_heads,), F32)
            o_a = _gdn(proj, gates, ab_conv_qkv[j], _pad_lanes([zeros_h, ab_a_log[j]]),
                       _pad_lanes([zeros_h, ab_dt_bias[j]]), ab_out_norm[j], batch=batch, n_heads=gdn_heads)
            o_b = _lru(proj, ab_lru_conv_w[j], ab_lru_conv_b[j], ab_lru_wa[j].astype(BF16), ab_lru_ba[j],
                       ab_lru_wx[j].astype(BF16), ab_lru_bx[j], ab_lru_lambda[j], batch=batch,
                       col_off=4 * gdn_heads)
            h = _outproj(h, o_a, o_b, ab_w_out[j].astype(BF16))
        else:
            w_main, w_gate = _split_w_in(cd_w_in, j, cd_sizes, (0, 1, 2, 4, 5, 6), (3,))
            proj, gates = _inproj(h, cd_norm[j], w_main, w_gate)
            o_c = _fox(proj, gates, _pad_lanes([cd_f_bias[j]]), batch=batch, n_heads=fox_heads)
            o_d = _moba(proj, batch=batch, n_heads=moba_heads, col_off=3 * fox_heads)
            h = _outproj(h, o_c, o_d, cd_w_out[j].astype(BF16))
        h = _ffn(h, ffn_norm[layer], w_gate16, w_up16, w_down16, final_norm, layer=layer,
                 final_norm=(layer == depth - 1))
    return h.reshape(batch, s_len, d_model)
```

```python
import functools
import math

import jax
import jax.numpy as jnp
from jax import lax
from jax.experimental import pallas as pl
from jax.experimental.pallas import tpu as pltpu

F32 = jnp.float32
BF16 = jnp.bfloat16

HEAD_DIM = 128
LANES = 128
NORM_EPS = 1e-6
CONV_WIDTH = 4
GDN_CHUNK = 64
GDN_GROUP = 8
GDN_HEADS_PER_STEP = 2
GDN_STAGES = 10
LRU_C = 8.0
MOBA_BLOCK = 256
MOBA_TOPK = 3
ROPE_THETA = 500000.0
ROPE_DIM = HEAD_DIM // 4
ATTN_Q_TILE = 256
VMEM_LIMIT_BYTES = 48 * 1024 * 1024
FFN_VMEM_LIMIT_BYTES = 58 * 1024 * 1024


def _cparams(n_axes, vmem_limit_bytes=VMEM_LIMIT_BYTES):
    return pltpu.CompilerParams(dimension_semantics=("arbitrary",) * n_axes,
                                vmem_limit_bytes=vmem_limit_bytes)


def _dot(a, b):
    return jnp.dot(a, b, preferred_element_type=F32)


def _dot_nt(a, b):
    return lax.dot_general(a, b, (((1,), (1,)), ((), ())), preferred_element_type=F32)


def _dot_tn(a, b):
    return lax.dot_general(a, b, (((0,), (0,)), ((), ())), preferred_element_type=F32)


def _rmsnorm(x, w):
    ms = jnp.mean(x * x, axis=-1, keepdims=True)
    return x * lax.rsqrt(ms + NORM_EPS) * w


def _sigmoid(x):
    return 0.5 * jnp.tanh(0.5 * x) + 0.5


def _softplus(x):
    return jnp.maximum(x, 0.0) + jnp.log1p(jnp.exp(-jnp.abs(x)))


def _silu(x):
    half = 0.5 * x
    return half + half * jnp.tanh(half)


CONV_PAD = 8


def _causal_conv(x, w, pad_ref, r0=0):
    rows = x.shape[0]
    k = w.shape[0]
    base = CONV_PAD + r0
    pad_ref[base:base + rows, :] = x
    acc = x * w[k - 1:k, :]
    for j in range(k - 1):
        acc = acc + pad_ref[base - (k - 1 - j):base - (k - 1 - j) + rows, :] * w[j:j + 1, :]
    return acc


def _lane_column(x, idx):
    lane = lax.broadcasted_iota(jnp.int32, x.shape, 1)
    return jnp.sum(jnp.where(lane == idx, x, 0.0), axis=-1, keepdims=True)


def _lane_cumsum(x, period):
    pos = lax.broadcasted_iota(jnp.int32, x.shape, 1) % period
    sh = 1
    while sh < period:
        x = x + jnp.where(pos >= sh, pltpu.roll(x, sh, 1), 0.0)
        sh *= 2
    return x


def _rows_to_columns(rows8, n_lanes):
    pad = jnp.zeros((n_lanes - rows8.shape[0], rows8.shape[1]), F32)
    return jnp.concatenate([rows8, pad], axis=0).T


def _inproj_kernel(x_ref, nw_ref, w_ref, wg_ref, o_ref, og_ref, xn_ref):
    @pl.when(pl.program_id(1) == 0)
    def _():
        xn = _rmsnorm(x_ref[...], nw_ref[...]).astype(BF16)
        xn_ref[...] = xn
        og_ref[...] = _dot(xn, wg_ref[...])

    o_ref[...] = _dot(xn_ref[...], w_ref[...]).astype(o_ref.dtype)


def _inproj(x, nw, w, wg, *, tm=1024, tn=1024):
    t, d = x.shape
    n = w.shape[1]
    return pl.pallas_call(
        _inproj_kernel,
        grid=(t // tm, n // tn),
        in_specs=[pl.BlockSpec((tm, d), lambda i, j: (i, 0)),
                  pl.BlockSpec((1, d), lambda i, j: (0, 0)),
                  pl.BlockSpec((d, tn), lambda i, j: (0, j)),
                  pl.BlockSpec((d, LANES), lambda i, j: (0, 0))],
        out_specs=[pl.BlockSpec((tm, tn), lambda i, j: (i, j)),
                   pl.BlockSpec((tm, LANES), lambda i, j: (i, 0))],
        out_shape=[jax.ShapeDtypeStruct((t, n), BF16), jax.ShapeDtypeStruct((t, LANES), F32)],
        scratch_shapes=[pltpu.VMEM((tm, d), BF16)],
        compiler_params=_cparams(2),
        name="inproj",
    )(x, nw.reshape(1, d), w, wg)


def _outproj_kernel(res_ref, a1_ref, a2_ref, w1_ref, w2_ref, o_ref):
    o_ref[...] = res_ref[...] + _dot(a1_ref[...], w1_ref[...]) + _dot(a2_ref[...], w2_ref[...])


def _outproj(res, a1, a2, w, *, tm=512, tn=2048):
    t, d = res.shape
    k1, k2 = a1.shape[1], a2.shape[1]
    assert k1 == k2 and w.shape == (k1 + k2, d)
    return pl.pallas_call(
        _outproj_kernel,
        grid=(t // tm, d // tn),
        in_specs=[pl.BlockSpec((tm, tn), lambda i, j: (i, j)),
                  pl.BlockSpec((tm, k1), lambda i, j: (i, 0)),
                  pl.BlockSpec((tm, k2), lambda i, j: (i, 0)),
                  pl.BlockSpec((k1, tn), lambda i, j: (0, j)),
                  pl.BlockSpec((k2, tn), lambda i, j: (1, j))],
        out_specs=pl.BlockSpec((tm, tn), lambda i, j: (i, j)),
        out_shape=jax.ShapeDtypeStruct((t, d), F32),
        compiler_params=_cparams(2),
        name="outproj",
    )(res, a1, a2, w, w)


def _ffn_kernel(x_ref, nw_ref, wg_ref, wu_ref, wd_ref, fnw_ref, o_ref, xn_ref, *, final_norm, down_chunk):
    f = pl.program_id(1)

    @pl.when(f == 0)
    def _():
        x = x_ref[...]
        xn_ref[...] = _rmsnorm(x, nw_ref[...]).astype(BF16)
        o_ref[...] = x

    xn = xn_ref[...]
    act = (_silu(_dot(xn, wg_ref[...])) * _dot(xn, wu_ref[...])).astype(BF16)
    for c0 in range(0, o_ref.shape[1], down_chunk):
        o_ref[:, c0:c0 + down_chunk] += _dot(act, wd_ref[:, c0:c0 + down_chunk])

    if final_norm:
        @pl.when(f == pl.num_programs(1) - 1)
        def _():
            o_ref[...] = _rmsnorm(o_ref[...], fnw_ref[...])


def _ffn(x, nw, wg, wu, wd, fnw, *, layer, final_norm, tm=1024, tf=512, down_chunk=512):
    t, d = x.shape
    dff = wg.shape[2]
    return pl.pallas_call(
        functools.partial(_ffn_kernel, final_norm=final_norm, down_chunk=down_chunk),
        grid=(t // tm, dff // tf),
        in_specs=[pl.BlockSpec((tm, d), lambda i, f: (i, 0)),
                  pl.BlockSpec((1, d), lambda i, f: (0, 0)),
                  pl.BlockSpec((None, d, tf), lambda i, f: (layer, 0, f)),
                  pl.BlockSpec((None, d, tf), lambda i, f: (layer, 0, f)),
                  pl.BlockSpec((None, tf, d), lambda i, f: (layer, f, 0)),
                  pl.BlockSpec((1, d), lambda i, f: (0, 0))],
        out_specs=pl.BlockSpec((tm, d), lambda i, f: (i, 0)),
        out_shape=jax.ShapeDtypeStruct((t, d), F32),
        scratch_shapes=[pltpu.VMEM((tm, d), BF16)],
        compiler_params=_cparams(2, FFN_VMEM_LIMIT_BYTES),
        name="ffn",
    )(x, nw.reshape(1, d), wg, wu, wd, fnw.reshape(1, d))


def _unit_lower_inverses(ms):
    c = ms[0].shape[0]
    eye = (lax.broadcasted_iota(jnp.int32, (c, c), 0) == lax.broadcasted_iota(jnp.int32, (c, c), 1)).astype(F32)
    invs = [eye - m for m in ms]
    m16s = [m.astype(BF16) for m in ms]
    ps = [_dot(m16, m16) for m16 in m16s]
    yield
    span = 2
    while span < c:
        p16s = [p.astype(BF16) for p in ps]
        if span * 2 < c:
            boths = [_dot(jnp.concatenate([inv.astype(BF16), p16], axis=0), p16) for inv, p16 in zip(invs, p16s)]
            invs = [inv + both[:c] for inv, both in zip(invs, boths)]
            ps = [both[c:] for both in boths]
        else:
            invs = [inv + _dot(inv.astype(BF16), p16) for inv, p16 in zip(invs, p16s)]
        span *= 2
        yield
    return invs


def _gdn_kernel(q_ref, k_ref, v_ref, z_ref, g_ref, cq_ref, ck_ref, cv_ref, alog_ref, dtb_ref, onw_ref,
                o_ref, beta_s, gcc_s, gcr_s, q_s, k_s, v_s, bc_s, gc_s, p_s, b_s, qp_s, out_s, pad_s,
                *, n_heads, hp):
    hb = pl.program_id(1)
    s_len = q_ref.shape[0]
    c = GDN_CHUNK
    group = GDN_GROUP
    hd = HEAD_DIM

    @pl.when(hb == 0)
    def _():
        logits = g_ref[...]
        beta_s[...] = _sigmoid(logits)
        log_decay = -jnp.exp(alog_ref[...]) * _softplus(logits + dtb_ref[...])
        rows = log_decay.T[n_heads:2 * n_heads, :]
        rows = _lane_cumsum(rows, c)
        for r in range(n_heads):
            gcr_s[r] = rows[r:r + 1, :]
        gcc_s[...] = _rows_to_columns(rows, LANES)

    span = group * c
    n_groups = s_len // span
    streams = ((q_ref, cq_ref, q_s), (k_ref, ck_ref, k_s), (v_ref, cv_ref, v_s))
    for j in range(hp):
        for a in range(len(streams)):
            pad_s[j, a, :CONV_PAD, :] = jnp.zeros((CONV_PAD, hd), F32)

    def input_thunks(gi):
        rows = slice(gi * span, (gi + 1) * span)
        thunks = []
        for j in range(hp):
            cols = slice(j * hd, (j + 1) * hd)
            head = hb * hp + j

            def gate_columns(j=j, head=head):
                bc_s[j, rows, :] = jnp.broadcast_to(_lane_column(beta_s[rows, :], head), (span, hd))
                gc_s[j, rows, :] = jnp.broadcast_to(_lane_column(gcc_s[rows, :], head), (span, hd))

            thunks.append(gate_columns)
            for a, (src, cw_ref, dst) in enumerate(streams):
                def conv_stream(j=j, a=a, cols=cols, src=src, cw_ref=cw_ref, dst=dst):
                    x = src[rows, cols].astype(F32)
                    y = _silu(_causal_conv(x, cw_ref[:, cols], pad_s.at[j, a], gi * span))
                    if dst is q_s:
                        y = y * (lax.rsqrt(jnp.sum(y * y, axis=-1, keepdims=True) + NORM_EPS) * (hd ** -0.5))
                    elif dst is k_s:
                        y = y * lax.rsqrt(jnp.sum(y * y, axis=-1, keepdims=True) + NORM_EPS)
                    dst[j, rows, :] = y

                thunks.append(conv_stream)
        return thunks

    ri = lax.broadcasted_iota(jnp.int32, (c, c), 0)
    ci = lax.broadcasted_iota(jnp.int32, (c, c), 1)
    incl = ri >= ci
    strict = ri > ci

    def chunk_terms(gi):
        gc_rows = [gcr_s[hb * hp + j, :, gi * span:(gi + 1) * span] for j in range(hp)]
        units = [(j, g) for g in range(group) for j in range(hp)]
        rows_u, scores_u, decay_u, rhs_u, kdec_u, qdec_u = [], [], [], [], [], []
        for j, g in units:
            rows = slice((gi * group + g) * c, (gi * group + g + 1) * c)
            qc, kc, vc = q_s[j, rows, :], k_s[j, rows, :], v_s[j, rows, :]
            beta, gc = bc_s[j, rows, :], gc_s[j, rows, :]
            gc_row = gc_rows[j][:, g * c:(g + 1) * c]
            decay_u.append(jnp.where(incl, jnp.exp(jnp.where(incl, gc[:, :c] - gc_row, 0.0)), 0.0))
            k_beta = kc * beta
            e_gc = jnp.exp(gc)
            lhs = jnp.concatenate([k_beta.astype(BF16), qc.astype(BF16)], axis=0)
            scores_u.append(_dot_nt(lhs, kc.astype(BF16)))
            rhs_u.append(jnp.concatenate([(vc * beta).astype(BF16), (k_beta * e_gc).astype(BF16)], axis=1))
            kdec_u.append((kc * jnp.exp(gc[c - 1:c, :] - gc)).astype(BF16))
            qdec_u.append(qc * e_gc)
            rows_u.append(rows)
        yield
        ms = [jnp.where(strict, sc[:c] * dec, 0.0) for sc, dec in zip(scores_u, decay_u)]
        attn_u = [jnp.where(incl, sc[c:] * dec, 0.0).astype(BF16) for sc, dec in zip(scores_u, decay_u)]
        t_u = yield from _unit_lower_inverses(ms)
        uw_u = [_dot(t.astype(BF16), rhs).astype(BF16) for t, rhs in zip(t_u, rhs_u)]
        yield
        bp_u = [_dot_tn(kdec, uw) for kdec, uw in zip(kdec_u, uw_u)]
        yield
        aw_u = [_dot(attn, uw) for attn, uw in zip(attn_u, uw_u)]
        for (j, g), rows, bp, aw, qdec in zip(units, rows_u, bp_u, aw_u, qdec_u):
            n = gi * group + g
            b_s[j, n] = bp[:, :hd]
            p_s[j, n] = bp[:, hd:].astype(BF16)
            qp_s[j, rows, :] = (qdec - aw[:, hd:]).astype(BF16)
            out_s[j, rows, :] = aw[:, :hd]
        yield

    def recurrence_steps(gi, states):
        def step(n):
            rows = slice(n * c, (n + 1) * c)
            for j in range(hp):
                s16 = states[j].astype(BF16)
                out_s[j, rows, :] += _dot(qp_s[j, rows, :], s16)
                g_last = gc_s[j, (n + 1) * c - 1:(n + 1) * c, :]
                states[j] = states[j] * jnp.exp(g_last) - _dot(p_s[j, n], s16) + b_s[j, n]
        return [functools.partial(step, gi * group + g) for g in range(group)]

    def output_thunks(gi):
        rows = slice(gi * span, (gi + 1) * span)

        def finish(j):
            cols = slice(j * hd, (j + 1) * hd)
            o = _rmsnorm(out_s[j, rows, :], onw_ref[...]) * _silu(z_ref[rows, cols].astype(F32))
            o_ref[rows, cols] = o.astype(o_ref.dtype)
        return [functools.partial(finish, j) for j in range(hp)]

    def interleave(stages, side):
        side = list(side)
        per_stage = -(-len(side) // GDN_STAGES)
        for _ in stages:
            for _ in range(min(per_stage, len(side))):
                side.pop(0)()
        for thunk in side:
            thunk()

    states = [jnp.zeros((hd, hd), F32) for _ in range(hp)]
    for thunk in input_thunks(0):
        thunk()
    for gi in range(n_groups):
        side = []
        rec = recurrence_steps(gi - 1, states) if gi > 0 else []
        prep = input_thunks(gi + 1) if gi + 1 < n_groups else []
        done = output_thunks(gi - 2) if gi > 1 else []
        while rec or prep or done:
            for lst in (rec, prep, done):
                if lst:
                    side.append(lst.pop(0))
        interleave(chunk_terms(gi), side)
    for thunk in recurrence_steps(n_groups - 1, states):
        thunk()
    for gi in range(max(n_groups - 2, 0), n_groups):
        for thunk in output_thunks(gi):
            thunk()


def _gdn(proj, gates, conv_w, alog_pad, dtb_pad, out_norm_w, *, batch, n_heads, hp=GDN_HEADS_PER_STEP):
    t = proj.shape[0]
    s_len = t // batch
    hd = HEAD_DIM
    nblk = n_heads // hp
    n_chunks = s_len // GDN_CHUNK
    assert n_heads % hp == 0 and s_len % (GDN_GROUP * GDN_CHUNK) == 0
    seq = lambda off: pl.BlockSpec((s_len, hp * hd), lambda b, h: (b, off + h))
    cw = lambda off: pl.BlockSpec((CONV_WIDTH, hp * hd), lambda b, h: (0, off + h))
    row = pl.BlockSpec((1, LANES), lambda b, h: (0, 0))
    act = pltpu.VMEM((s_len, LANES), F32)
    per_head = pltpu.VMEM((hp, s_len, hd), F32)
    return pl.pallas_call(
        functools.partial(_gdn_kernel, n_heads=n_heads, hp=hp),
        grid=(batch, nblk),
        in_specs=[seq(0), seq(nblk), seq(2 * nblk), seq(3 * nblk),
                  pl.BlockSpec((s_len, LANES), lambda b, h: (b, 0)),
                  cw(0), cw(nblk), cw(2 * nblk), row, row, row],
        out_specs=pl.BlockSpec((s_len, hp * hd), lambda b, h: (b, h)),
        out_shape=jax.ShapeDtypeStruct((t, n_heads * hd), BF16),
        scratch_shapes=[act, act, pltpu.VMEM((n_heads, 1, s_len), F32),
                        per_head, per_head, per_head, per_head, per_head,
                        pltpu.VMEM((hp, n_chunks, hd, hd), BF16), pltpu.VMEM((hp, n_chunks, hd, hd), F32),
                        pltpu.VMEM((hp, s_len, hd), BF16), per_head,
                        pltpu.VMEM((hp, 3, CONV_PAD + s_len, hd), F32)],
        compiler_params=_cparams(2),
        name="gdn",
    )(proj, proj, proj, proj, gates, conv_w, conv_w, conv_w, alog_pad, dtb_pad, out_norm_w.reshape(1, hd))


def _gelu_tanh(x):
    return 0.5 * x * (1.0 + jnp.tanh(math.sqrt(2.0 / math.pi) * (x + 0.044715 * (x * x * x))))


def _lru_kernel(x_ref, y_ref, cw_ref, cb_ref, wa_ref, ba_ref, wx_ref, bx_ref, lam_ref, o_ref, pad_s):
    pad_s[:CONV_PAD, :] = jnp.zeros((CONV_PAD, pad_s.shape[1]), F32)
    xc = _causal_conv(x_ref[...].astype(F32), cw_ref[...], pad_s) + cb_ref[...]
    x16 = xc.astype(BF16)
    r = _sigmoid(_dot(x16, wa_ref[0]) + ba_ref[...])
    i = _sigmoid(_dot(x16, wx_ref[0]) + bx_ref[...])
    log_a = (-LRU_C) * r * _softplus(-lam_ref[...])
    a = jnp.exp(log_a)
    th = jnp.tanh(log_a)
    b = jnp.sqrt(-2.0 * th / (1.0 - th)) * i * xc
    row = lax.broadcasted_iota(jnp.int32, a.shape, 0)
    s_len, width = a.shape
    sublanes = 8

    def shifted(x, sh, fill):
        if sh % sublanes == 0:
            return jnp.concatenate([jnp.full((sh, width), fill, F32), x[:s_len - sh, :]], axis=0)
        return jnp.where(row >= sh, pltpu.roll(x, sh, 0), fill)

    sh = 1
    while sh < s_len:
        b = a * shifted(b, sh, 0.0) + b
        if sh * 2 < s_len:
            a = a * shifted(a, sh, 1.0)
        sh *= 2
    o_ref[...] = (b * _gelu_tanh(y_ref[...].astype(F32))).astype(o_ref.dtype)


def _lru(proj, conv_w, conv_b, wa, ba, wx, bx, lam, *, batch, col_off):
    t = proj.shape[0]
    s_len = t // batch
    n_blocks, blk = wa.shape[0], wa.shape[1]
    seq = lambda off: pl.BlockSpec((s_len, blk), lambda b, n: (b, off + n))
    vec = pl.BlockSpec((1, blk), lambda b, n: (0, n))
    mat = pl.BlockSpec((1, blk, blk), lambda b, n: (n, 0, 0))
    width = n_blocks * blk
    return pl.pallas_call(
        _lru_kernel,
        grid=(batch, n_blocks),
        in_specs=[seq(col_off), seq(col_off + n_blocks),
                  pl.BlockSpec((CONV_WIDTH, blk), lambda b, n: (0, n)), vec, mat, vec, mat, vec, vec],
        out_specs=pl.BlockSpec((s_len, blk), lambda b, n: (b, n)),
        out_shape=jax.ShapeDtypeStruct((t, width), BF16),
        scratch_shapes=[pltpu.VMEM((CONV_PAD + s_len, blk), F32)],
        compiler_params=_cparams(2),
        name="rglru",
    )(proj, proj, conv_w, conv_b.reshape(1, width), wa, ba.reshape(1, width), wx, bx.reshape(1, width),
      lam.reshape(1, width))


LOG2E = math.log2(math.e)


def _softmax2_pv(logits2, v16):
    mx = jnp.max(logits2, axis=-1, keepdims=True)
    p = jnp.exp2(logits2 - mx)
    denom = jnp.sum(p, axis=-1, keepdims=True)
    return _dot(p.astype(BF16), v16) / denom


def _causal_tile(logits2, lo, extra=None):
    tq = logits2.shape[0]
    tri = lax.broadcasted_iota(jnp.int32, (tq, tq), 1) <= lax.broadcasted_iota(jnp.int32, (tq, tq), 0)
    parts = [] if lo == 0 else ([logits2[:, :lo]] if extra is None else extra)
    parts.append(jnp.where(tri, logits2[:, lo:], -jnp.inf))
    return parts[0] if len(parts) == 1 else jnp.concatenate(parts, axis=1)


def _fox_kernel(q_ref, k_ref, v_ref, g_ref, fb_ref, o_ref, cfc_s, cfr_s, *, n_heads):
    h = pl.program_id(1)
    s_len = q_ref.shape[0]
    tq = ATTN_Q_TILE

    @pl.when(h == 0)
    def _():
        z = g_ref[...] + fb_ref[...]
        log_f = -_softplus(-z)
        rows = _lane_cumsum(log_f.T[0:n_heads, :], s_len)
        for r in range(n_heads):
            cfr_s[r] = rows[r:r + 1, :]
        cfc_s[...] = _rows_to_columns(rows, LANES)

    cf_col2 = _lane_column(cfc_s[...], h) * LOG2E
    cf_row2 = cfr_s[h] * LOG2E
    k16 = k_ref[...]
    v16 = v_ref[...]
    scale2 = (HEAD_DIM ** -0.5) * LOG2E
    n_tiles = s_len // tq
    qk = lambda i: _dot_nt(q_ref[i * tq:(i + 1) * tq, :], k16[:(i + 1) * tq, :])
    qk_next = qk(0)
    for i in range(n_tiles):
        lo, hi = i * tq, (i + 1) * tq
        qk_cur, qk_next = qk_next, (qk(i + 1) if i + 1 < n_tiles else None)
        logits2 = qk_cur * scale2 + cf_col2[lo:hi, :] - cf_row2[:, :hi]
        o_ref[lo:hi, :] = _softmax2_pv(_causal_tile(logits2, lo), v16[:hi, :]).astype(o_ref.dtype)


def _fox(proj, gates, fb_pad, *, batch, n_heads):
    t = proj.shape[0]
    s_len = t // batch
    hd = HEAD_DIM
    seq = lambda off: pl.BlockSpec((s_len, hd), lambda b, h: (b, off + h))
    return pl.pallas_call(
        functools.partial(_fox_kernel, n_heads=n_heads),
        grid=(batch, n_heads),
        in_specs=[seq(0), seq(n_heads), seq(2 * n_heads),
                  pl.BlockSpec((s_len, LANES), lambda b, h: (b, 0)),
                  pl.BlockSpec((1, LANES), lambda b, h: (0, 0))],
        out_specs=pl.BlockSpec((s_len, hd), lambda b, h: (b, h)),
        out_shape=jax.ShapeDtypeStruct((t, n_heads * hd), BF16),
        scratch_shapes=[pltpu.VMEM((s_len, LANES), F32), pltpu.VMEM((n_heads, 1, s_len), F32)],
        compiler_params=_cparams(2),
        name="fox",
    )(proj, proj, proj, gates, fb_pad)


def _rotary(x, cos_t, sin_lo, sin_hi):
    half = ROPE_DIM // 2
    return x * cos_t + pltpu.roll(x, LANES - half, 1) * sin_lo + pltpu.roll(x, half, 1) * sin_hi


def _moba_kernel(q_ref, k_ref, v_ref, cos_ref, slo_ref, shi_ref, o_ref, q_s, sel_s):
    s_len = q_ref.shape[0]
    blk = MOBA_BLOCK
    nb = s_len // blk
    cos_t, sin_lo, sin_hi = cos_ref[...], slo_ref[...], shi_ref[...]
    q = _rotary(q_ref[...].astype(F32), cos_t, sin_lo, sin_hi)
    k = _rotary(k_ref[...].astype(F32), cos_t, sin_lo, sin_hi)
    q_s[...] = q.astype(BF16)
    k16 = k.astype(BF16)
    v16 = v_ref[...]

    nb_pad = -(-nb // 8) * 8
    k_mean = jnp.concatenate([jnp.mean(k[n * blk:(n + 1) * blk, :], axis=0, keepdims=True) for n in range(nb)]
                             + ([jnp.zeros((nb_pad - nb, HEAD_DIM), F32)] if nb_pad > nb else []), axis=0)
    q_hi = q_s[...]
    q_lo = (q - q_hi.astype(F32)).astype(BF16)
    km_hi = k_mean.astype(BF16)
    km_lo = (k_mean - km_hi.astype(F32)).astype(BF16)
    gate = (_dot_nt(km_hi, q_hi) + _dot_nt(km_hi, q_lo)) + (_dot_nt(km_lo, q_hi) + _dot_nt(km_lo, q_lo))
    blk_row = lax.broadcasted_iota(jnp.int32, gate.shape, 0)
    q_blk = lax.broadcasted_iota(jnp.int32, gate.shape, 1) // blk
    past = blk_row < q_blk
    gate = jnp.where(past, gate, -jnp.inf)
    rank = jnp.zeros(gate.shape, F32)
    for m in range(nb - 1):
        g_m = gate[m:m + 1, :]
        ahead = (g_m > gate) | ((g_m == gate) & (blk_row > m))
        rank = rank + jnp.where(ahead & (q_blk > m), 1.0, 0.0)
    keep_bias = jnp.where(past & (rank < MOBA_TOPK), 0.0, -jnp.inf)
    sel_s[...] = _rows_to_columns(keep_bias, LANES)

    scale2 = (HEAD_DIM ** -0.5) * LOG2E
    for i in range(nb):
        lo, hi = i * blk, (i + 1) * blk
        logits2 = _dot_nt(q_s[lo:hi, :], k16[:hi, :]) * scale2
        sel = sel_s[lo:hi, :]
        extra = [logits2[:, j * blk:(j + 1) * blk] + sel[:, j:j + 1] for j in range(i)]
        o_ref[lo:hi, :] = _softmax2_pv(_causal_tile(logits2, lo, extra), v16[:hi, :]).astype(o_ref.dtype)


def _rope_tables(s_len):
    half = ROPE_DIM // 2
    inv_freq = ROPE_THETA ** (-jnp.arange(half, dtype=F32) / half)
    ang = jnp.arange(s_len, dtype=F32)[:, None] * inv_freq[None, :]
    cos, sin = jnp.cos(ang), jnp.sin(ang)
    zeros = jnp.zeros((s_len, HEAD_DIM - ROPE_DIM), F32)
    z_half = jnp.zeros((s_len, half), F32)
    cos_t = jnp.concatenate([cos, cos, zeros + 1.0], axis=1)
    sin_lo = jnp.concatenate([-sin, z_half, zeros], axis=1)
    sin_hi = jnp.concatenate([z_half, sin, zeros], axis=1)
    return cos_t, sin_lo, sin_hi


def _moba(proj, *, batch, n_heads, col_off):
    t = proj.shape[0]
    s_len = t // batch
    hd = HEAD_DIM
    seq = lambda off: pl.BlockSpec((s_len, hd), lambda b, h: (b, off + h))
    tab = pl.BlockSpec((s_len, hd), lambda b, h: (0, 0))
    return pl.pallas_call(
        _moba_kernel,
        grid=(batch, n_heads),
        in_specs=[seq(col_off), seq(col_off + n_heads), seq(col_off + 2 * n_heads), tab, tab, tab],
        out_specs=pl.BlockSpec((s_len, hd), lambda b, h: (b, h)),
        out_shape=jax.ShapeDtypeStruct((t, n_heads * hd), BF16),
        scratch_shapes=[pltpu.VMEM((s_len, hd), BF16), pltpu.VMEM((s_len, LANES), F32)],
        compiler_params=_cparams(2),
        name="moba",
    )(proj, proj, proj, *_rope_tables(s_len))


def _pad_lanes(parts):
    row = jnp.concatenate([p.reshape(1, -1).astype(F32) for p in parts], axis=1)
    return jnp.pad(row, ((0, 0), (0, LANES - row.shape[1])))


def _regroup_kernel(w_ref, main_ref, gate_ref, *, main_runs, gate_runs):
    w = w_ref[...]
    main_ref[...] = jnp.concatenate([w[:, a:b] for a, b in main_runs], axis=1).astype(main_ref.dtype)
    gate = [w[:, a:b] for a, b in gate_runs]
    n_gate = sum(b - a for a, b in gate_runs)
    gate.append(jnp.zeros((w.shape[0], gate_ref.shape[1] - n_gate), w.dtype))
    gate_ref[...] = jnp.concatenate(gate, axis=1).astype(gate_ref.dtype)


def _split_w_in(w, layer, sizes, main_idx, gate_idx, *, tr=256):
    offs = [0]
    for n in sizes:
        offs.append(offs[-1] + n)

    def runs(idx):
        out = []
        for i in idx:
            if out and out[-1][1] == offs[i]:
                out[-1] = (out[-1][0], offs[i + 1])
            else:
                out.append((offs[i], offs[i + 1]))
        return tuple(out)

    _, d, n_in = w.shape
    n_main = sum(sizes[i] for i in main_idx)
    return pl.pallas_call(
        functools.partial(_regroup_kernel, main_runs=runs(main_idx), gate_runs=runs(gate_idx)),
        grid=(d // tr,),
        in_specs=[pl.BlockSpec((None, tr, n_in), lambda i: (layer, i, 0))],
        out_specs=[pl.BlockSpec((tr, n_main), lambda i: (i, 0)), pl.BlockSpec((tr, LANES), lambda i: (i, 0))],
        out_shape=[jax.ShapeDtypeStruct((d, n_main), BF16), jax.ShapeDtypeStruct((d, LANES), BF16)],
        compiler_params=_cparams(1),
        name="regroup_w_in",
    )(w)


def kernel(x, ab_norm, ab_w_in, ab_conv_qkv, ab_a_log, ab_dt_bias, ab_out_norm, ab_lru_conv_w, ab_lru_conv_b,
           ab_lru_wa, ab_lru_ba, ab_lru_wx, ab_lru_bx, ab_lru_lambda, ab_w_out, cd_norm, cd_w_in, cd_f_bias,
           cd_w_out, ffn_norm, ffn_w_gate, ffn_w_up, ffn_w_down, final_norm):
    batch, s_len, d_model = x.shape
    depth = ffn_norm.shape[0]
    gdn_heads = ab_a_log.shape[1]
    gdn_width = gdn_heads * HEAD_DIM
    lru_width = ab_lru_lambda.shape[1]
    fox_heads = cd_f_bias.shape[1]
    fox_width = fox_heads * HEAD_DIM
    moba_width = d_model - fox_width
    moba_heads = moba_width // HEAD_DIM
    ab_sizes = (gdn_width,) * 4 + (gdn_heads,) * 2 + (lru_width,) * 2
    cd_sizes = (fox_width,) * 3 + (fox_heads,) + (moba_width,) * 3

    w_gate16, w_up16, w_down16 = (w.astype(BF16) for w in (ffn_w_gate, ffn_w_up, ffn_w_down))
    h = x.reshape(batch * s_len, d_model)
    for layer in range(depth):
        j = layer // 2
        if layer % 2 == 0:
            w_main, w_gate = _split_w_in(ab_w_in, j, ab_sizes, (0, 1, 2, 3, 6, 7), (4, 5))
            proj, gates = _inproj(h, ab_norm[j], w_main, w_gate)
            zeros_h = jnp.zeros((gdn_heads,), F32)
            o_a = _gdn(proj, gates, ab_conv_qkv[j], _pad_lanes([zeros_h, ab_a_log[j]]),
                       _pad_lanes([zeros_h, ab_dt_bias[j]]), ab_out_norm[j], batch=batch, n_heads=gdn_heads)
            o_b = _lru(proj, ab_lru_conv_w[j], ab_lru_conv_b[j], ab_lru_wa[j].astype(BF16), ab_lru_ba[j],
                       ab_lru_wx[j].astype(BF16), ab_lru_bx[j], ab_lru_lambda[j], batch=batch,
                       col_off=4 * gdn_heads)
            h = _outproj(h, o_a, o_b, ab_w_out[j].astype(BF16))
        else:
            w_main, w_gate = _split_w_in(cd_w_in, j, cd_sizes, (0, 1, 2, 4, 5, 6), (3,))
            proj, gates = _inproj(h, cd_norm[j], w_main, w_gate)
            o_c = _fox(proj, gates, _pad_lanes([cd_f_bias[j]]), batch=batch, n_heads=fox_heads)
            o_d = _moba(proj, batch=batch, n_heads=moba_heads, col_off=3 * fox_heads)
            h = _outproj(h, o_c, o_d, cd_w_out[j].astype(BF16))
        h = _ffn(h, ffn_norm[layer], w_gate16, w_up16, w_down16, final_norm, layer=layer,
                 final_norm=(layer == depth - 1))
    return h.reshape(batch, s_len, d_model)
```

```python
import functools
import math

import jax
import jax.numpy as jnp
from jax import lax
from jax.experimental import pallas as pl
from jax.experimental.pallas import tpu as pltpu

F32 = jnp.float32
BF16 = jnp.bfloat16

HEAD_DIM = 128
LANES = 128
NORM_EPS = 1e-6
CONV_WIDTH = 4
GDN_CHUNK = 64
GDN_GROUP = 8
GDN_HEADS_PER_STEP = 2
GDN_STAGES = 10
LRU_C = 8.0
MOBA_BLOCK = 256
MOBA_TOPK = 3
ROPE_THETA = 500000.0
ROPE_DIM = HEAD_DIM // 4
ATTN_Q_TILE = 256
VMEM_LIMIT_BYTES = 48 * 1024 * 1024
FFN_VMEM_LIMIT_BYTES = 58 * 1024 * 1024


def _cparams(n_axes, vmem_limit_bytes=VMEM_LIMIT_BYTES):
    return pltpu.CompilerParams(dimension_semantics=("arbitrary",) * n_axes,
                                vmem_limit_bytes=vmem_limit_bytes)


def _dot(a, b):
    return jnp.dot(a, b, preferred_element_type=F32)


def _dot_nt(a, b):
    return lax.dot_general(a, b, (((1,), (1,)), ((), ())), preferred_element_type=F32)


def _dot_tn(a, b):
    return lax.dot_general(a, b, (((0,), (0,)), ((), ())), preferred_element_type=F32)


def _rmsnorm(x, w):
    ms = jnp.mean(x * x, axis=-1, keepdims=True)
    return x * lax.rsqrt(ms + NORM_EPS) * w


def _sigmoid(x):
    return 0.5 * jnp.tanh(0.5 * x) + 0.5


def _softplus(x):
    return jnp.maximum(x, 0.0) + jnp.log1p(jnp.exp(-jnp.abs(x)))


def _silu(x):
    half = 0.5 * x
    return half + half * jnp.tanh(half)


CONV_PAD = 8


def _causal_conv(x, w, pad_ref, r0=0):
    rows = x.shape[0]
    k = w.shape[0]
    base = CONV_PAD + r0
    pad_ref[base:base + rows, :] = x
    acc = x * w[k - 1:k, :]
    for j in range(k - 1):
        acc = acc + pad_ref[base - (k - 1 - j):base - (k - 1 - j) + rows, :] * w[j:j + 1, :]
    return acc


def _lane_column(x, idx):
    lane = lax.broadcasted_iota(jnp.int32, x.shape, 1)
    return jnp.sum(jnp.where(lane == idx, x, 0.0), axis=-1, keepdims=True)


def _lane_cumsum(x, period):
    pos = lax.broadcasted_iota(jnp.int32, x.shape, 1) % period
    sh = 1
    while sh < period:
        x = x + jnp.where(pos >= sh, pltpu.roll(x, sh, 1), 0.0)
        sh *= 2
    return x


def _rows_to_columns(rows8, n_lanes):
    pad = jnp.zeros((n_lanes - rows8.shape[0], rows8.shape[1]), F32)
    return jnp.concatenate([rows8, pad], axis=0).T


def _inproj_kernel(x_ref, nw_ref, w_ref, wg_ref, o_ref, og_ref, xn_ref):
    @pl.when(pl.program_id(1) == 0)
    def _():
        xn = _rmsnorm(x_ref[...], nw_ref[...]).astype(BF16)
        xn_ref[...] = xn
        og_ref[...] = _dot(xn, wg_ref[...])

    o_ref[...] = _dot(xn_ref[...], w_ref[...]).astype(o_ref.dtype)


def _inproj(x, nw, w, wg, *, tm=1024, tn=1024):
    t, d = x.shape
    n = w.shape[1]
    return pl.pallas_call(
        _inproj_kernel,
        grid=(t // tm, n // tn),
        in_specs=[pl.BlockSpec((tm, d), lambda i, j: (i, 0)),
                  pl.BlockSpec((1, d), lambda i, j: (0, 0)),
                  pl.BlockSpec((d, tn), lambda i, j: (0, j)),
                  pl.BlockSpec((d, LANES), lambda i, j: (0, 0))],
        out_specs=[pl.BlockSpec((tm, tn), lambda i, j: (i, j)),
                   pl.BlockSpec((tm, LANES), lambda i, j: (i, 0))],
        out_shape=[jax.ShapeDtypeStruct((t, n), BF16), jax.ShapeDtypeStruct((t, LANES), F32)],
        scratch_shapes=[pltpu.VMEM((tm, d), BF16)],
        compiler_params=_cparams(2),
        name="inproj",
    )(x, nw.reshape(1, d), w, wg)


def _outproj_kernel(res_ref, a1_ref, a2_ref, w1_ref, w2_ref, o_ref):
    o_ref[...] = res_ref[...] + _dot(a1_ref[...], w1_ref[...]) + _dot(a2_ref[...], w2_ref[...])


def _outproj(res, a1, a2, w, *, tm=512, tn=2048):
    t, d = res.shape
    k1, k2 = a1.shape[1], a2.shape[1]
    assert k1 == k2 and w.shape == (k1 + k2, d)
    return pl.pallas_call(
        _outproj_kernel,
        grid=(t // tm, d // tn),
        in_specs=[pl.BlockSpec((tm, tn), lambda i, j: (i, j)),
                  pl.BlockSpec((tm, k1), lambda i, j: (i, 0)),
                  pl.BlockSpec((tm, k2), lambda i, j: (i, 0)),
                  pl.BlockSpec((k1, tn), lambda i, j: (0, j)),
                  pl.BlockSpec((k2, tn), lambda i, j: (1, j))],
        out_specs=pl.BlockSpec((tm, tn), lambda i, j: (i, j)),
        out_shape=jax.ShapeDtypeStruct((t, d), F32),
        compiler_params=_cparams(2),
        name="outproj",
    )(res, a1, a2, w, w)


def _ffn_kernel(x_ref, nw_ref, wg_ref, wu_ref, wd_ref, fnw_ref, o_ref, xn_ref, *, final_norm, down_chunk):
    f = pl.program_id(1)

    @pl.when(f == 0)
    def _():
        x = x_ref[...]
        xn_ref[...] = _rmsnorm(x, nw_ref[...]).astype(BF16)
        o_ref[...] = x

    xn = xn_ref[...]
    act = (_silu(_dot(xn, wg_ref[...])) * _dot(xn, wu_ref[...])).astype(BF16)
    for c0 in range(0, o_ref.shape[1], down_chunk):
        o_ref[:, c0:c0 + down_chunk] += _dot(act, wd_ref[:, c0:c0 + down_chunk])

    if final_norm:
        @pl.when(f == pl.num_programs(1) - 1)
        def _():
            o_ref[...] = _rmsnorm(o_ref[...], fnw_ref[...])


def _ffn(x, nw, wg, wu, wd, fnw, *, layer, final_norm, tm=1024, tf=512, down_chunk=512):
    t, d = x.shape
    dff = wg.shape[2]
    return pl.pallas_call(
        functools.partial(_ffn_kernel, final_norm=final_norm, down_chunk=down_chunk),
        grid=(t // tm, dff // tf),
        in_specs=[pl.BlockSpec((tm, d), lambda i, f: (i, 0)),
                  pl.BlockSpec((1, d), lambda i, f: (0, 0)),
                  pl.BlockSpec((None, d, tf), lambda i, f: (layer, 0, f)),
                  pl.BlockSpec((None, d, tf), lambda i, f: (layer, 0, f)),
                  pl.BlockSpec((None, tf, d), lambda i, f: (layer, f, 0)),
                  pl.BlockSpec((1, d), lambda i, f: (0, 0))],
        out_specs=pl.BlockSpec((tm, d), lambda i, f: (i, 0)),
        out_shape=jax.ShapeDtypeStruct((t, d), F32),
        scratch_shapes=[pltpu.VMEM((tm, d), BF16)],
        compiler_params=_cparams(2, FFN_VMEM_LIMIT_BYTES),
        name="ffn",
    )(x, nw.reshape(1, d), wg, wu, wd, fnw.reshape(1, d))


def _unit_lower_inverses(ms):
    c = ms[0].shape[0]
    eye = (lax.broadcasted_iota(jnp.int32, (c, c), 0) == lax.broadcasted_iota(jnp.int32, (c, c), 1)).astype(F32)
    invs = [eye - m for m in ms]
    m16s = [m.astype(BF16) for m in ms]
    ps = [_dot(m16, m16) for m16 in m16s]
    yield
    span = 2
    while span < c:
        p16s = [p.astype(BF16) for p in ps]
        if span * 2 < c:
            boths = [_dot(jnp.concatenate([inv.astype(BF16), p16], axis=0), p16) for inv, p16 in zip(invs, p16s)]
            invs = [inv + both[:c] for inv, both in zip(invs, boths)]
            ps = [both[c:] for both in boths]
        else:
            invs = [inv + _dot(inv.astype(BF16), p16) for inv, p16 in zip(invs, p16s)]
        span *= 2
        yield
    return invs


def _gdn_kernel(q_ref, k_ref, v_ref, z_ref, g_ref, cq_ref, ck_ref, cv_ref, alog_ref, dtb_ref, onw_ref,
                o_ref, beta_s, gcc_s, gcr_s, q_s, k_s, v_s, bc_s, gc_s, p_s, b_s, qp_s, out_s, pad_s,
                *, n_heads, hp):
    hb = pl.program_id(1)
    s_len = q_ref.shape[0]
    c = GDN_CHUNK
    group = GDN_GROUP
    hd = HEAD_DIM

    @pl.when(hb == 0)
    def _():
        logits = g_ref[...]
        beta_s[...] = _sigmoid(logits)
        log_decay = -jnp.exp(alog_ref[...]) * _softplus(logits + dtb_ref[...])
        rows = log_decay.T[n_heads:2 * n_heads, :]
        rows = _lane_cumsum(rows, c)
        for r in range(n_heads):
            gcr_s[r] = rows[r:r + 1, :]
        gcc_s[...] = _rows_to_columns(rows, LANES)

    span = group * c
    n_groups = s_len // span
    streams = ((q_ref, cq_ref, q_s), (k_ref, ck_ref, k_s), (v_ref, cv_ref, v_s))
    for j in range(hp):
        for a in range(len(streams)):
            pad_s[j, a, :CONV_PAD, :] = jnp.zeros((CONV_PAD, hd), F32)

    def input_thunks(gi):
        rows = slice(gi * span, (gi + 1) * span)
        thunks = []
        for j in range(hp):
            cols = slice(j * hd, (j + 1) * hd)
            head = hb * hp + j

            def gate_columns(j=j, head=head):
                bc_s[j, rows, :] = jnp.broadcast_to(_lane_column(beta_s[rows, :], head), (span, hd))
                gc_s[j, rows, :] = jnp.broadcast_to(_lane_column(gcc_s[rows, :], head), (span, hd))

            thunks.append(gate_columns)
            for a, (src, cw_ref, dst) in enumerate(streams):
                def conv_stream(j=j, a=a, cols=cols, src=src, cw_ref=cw_ref, dst=dst):
                    x = src[rows, cols].astype(F32)
                    y = _silu(_causal_conv(x, cw_ref[:, cols], pad_s.at[j, a], gi * span))
                    if dst is q_s:
                        y = y * (lax.rsqrt(jnp.sum(y * y, axis=-1, keepdims=True) + NORM_EPS) * (hd ** -0.5))
                    elif dst is k_s:
                        y = y * lax.rsqrt(jnp.sum(y * y, axis=-1, keepdims=True) + NORM_EPS)
                    dst[j, rows, :] = y

                thunks.append(conv_stream)
        return thunks

    ri = lax.broadcasted_iota(jnp.int32, (c, c), 0)
    ci = lax.broadcasted_iota(jnp.int32, (c, c), 1)
    incl = ri >= ci
    strict = ri > ci

    def chunk_terms(gi):
        gc_rows = [gcr_s[hb * hp + j, :, gi * span:(gi + 1) * span] for j in range(hp)]
        units = [(j, g) for g in range(group) for j in range(hp)]
        rows_u, scores_u, decay_u, rhs_u, kdec_u, qdec_u = [], [], [], [], [], []
        for j, g in units:
            rows = slice((gi * group + g) * c, (gi * group + g + 1) * c)
            qc, kc, vc = q_s[j, rows, :], k_s[j, rows, :], v_s[j, rows, :]
            beta, gc = bc_s[j, rows, :], gc_s[j, rows, :]
            gc_row = gc_rows[j][:, g * c:(g + 1) * c]
            decay_u.append(jnp.where(incl, jnp.exp(jnp.where(incl, gc[:, :c] - gc_row, 0.0)), 0.0))
            k_beta = kc * beta
            e_gc = jnp.exp(gc)
            lhs = jnp.concatenate([k_beta.astype(BF16), qc.astype(BF16)], axis=0)
            scores_u.append(_dot_nt(lhs, kc.astype(BF16)))
            rhs_u.append(jnp.concatenate([(vc * beta).astype(BF16), (k_beta * e_gc).astype(BF16)], axis=1))
            kdec_u.append((kc * jnp.exp(gc[c - 1:c, :] - gc)).astype(BF16))
            qdec_u.append(qc * e_gc)
            rows_u.append(rows)
        yield
        ms = [jnp.where(strict, sc[:c] * dec, 0.0) for sc, dec in zip(scores_u, decay_u)]
        attn_u = [jnp.where(incl, sc[c:] * dec, 0.0).astype(BF16) for sc, dec in zip(scores_u, decay_u)]
        t_u = yield from _unit_lower_inverses(ms)
        uw_u = [_dot(t.astype(BF16), rhs).astype(BF16) for t, rhs in zip(t_u, rhs_u)]
        yield
        bp_u = [_dot_tn(kdec, uw) for kdec, uw in zip(kdec_u, uw_u)]
        yield
        aw_u = [_dot(attn, uw) for attn, uw in zip(attn_u, uw_u)]
        for (j, g), rows, bp, aw, qdec in zip(units, rows_u, bp_u, aw_u, qdec_u):
            n = gi * group + g
            b_s[j, n] = bp[:, :hd]
            p_s[j, n] = bp[:, hd:].astype(BF16)
            qp_s[j, rows, :] = (qdec - aw[:, hd:]).astype(BF16)
            out_s[j, rows, :] = aw[:, :hd]
        yield

    def recurrence_steps(gi, states):
        def step(n):
            rows = slice(n * c, (n + 1) * c)
            for j in range(hp):
                s16 = states[j].astype(BF16)
                out_s[j, rows, :] += _dot(qp_s[j, rows, :], s16)
                g_last = gc_s[j, (n + 1) * c - 1:(n + 1) * c, :]
                states[j] = states[j] * jnp.exp(g_last) - _dot(p_s[j, n], s16) + b_s[j, n]
        return [functools.partial(step, gi * group + g) for g in range(group)]

    def output_thunks(gi):
        rows = slice(gi * span, (gi + 1) * span)

        def finish(j):
            cols = slice(j * hd, (j + 1) * hd)
            o = _rmsnorm(out_s[j, rows, :], onw_ref[...]) * _silu(z_ref[rows, cols].astype(F32))
            o_ref[rows, cols] = o.astype(o_ref.dtype)
        return [functools.partial(finish, j) for j in range(hp)]

    def interleave(stages, side):
        side = list(side)
        per_stage = -(-len(side) // GDN_STAGES)
        for _ in stages:
            for _ in range(min(per_stage, len(side))):
                side.pop(0)()
        for thunk in side:
            thunk()

    states = [jnp.zeros((hd, hd), F32) for _ in range(hp)]
    for thunk in input_thunks(0):
        thunk()
    for gi in range(n_groups):
        side = []
        rec = recurrence_steps(gi - 1, states) if gi > 0 else []
        prep = input_thunks(gi + 1) if gi + 1 < n_groups else []
        done = output_thunks(gi - 2) if gi > 1 else []
        while rec or prep or done:
            for lst in (rec, prep, done):
                if lst:
                    side.append(lst.pop(0))
        interleave(chunk_terms(gi), side)
    for thunk in recurrence_steps(n_groups - 1, states):
        thunk()
    for gi in range(max(n_groups - 2, 0), n_groups):
        for thunk in output_thunks(gi):
            thunk()


def _gdn(proj, gates, conv_w, alog_pad, dtb_pad, out_norm_w, *, batch, n_heads, hp=GDN_HEADS_PER_STEP):
    t = proj.shape[0]
    s_len = t // batch
    hd = HEAD_DIM
    nblk = n_heads // hp
    n_chunks = s_len // GDN_CHUNK
    assert n_heads % hp == 0 and s_len % (GDN_GROUP * GDN_CHUNK) == 0
    seq = lambda off: pl.BlockSpec((s_len, hp * hd), lambda b, h: (b, off + h))
    cw = lambda off: pl.BlockSpec((CONV_WIDTH, hp * hd), lambda b, h: (0, off + h))
    row = pl.BlockSpec((1, LANES), lambda b, h: (0, 0))
    act = pltpu.VMEM((s_len, LANES), F32)
    per_head = pltpu.VMEM((hp, s_len, hd), F32)
    return pl.pallas_call(
        functools.partial(_gdn_kernel, n_heads=n_heads, hp=hp),
        grid=(batch, nblk),
        in_specs=[seq(0), seq(nblk), seq(2 * nblk), seq(3 * nblk),
                  pl.BlockSpec((s_len, LANES), lambda b, h: (b, 0)),
                  cw(0), cw(nblk), cw(2 * nblk), row, row, row],
        out_specs=pl.BlockSpec((s_len, hp * hd), lambda b, h: (b, h)),
        out_shape=jax.ShapeDtypeStruct((t, n_heads * hd), BF16),
        scratch_shapes=[act, act, pltpu.VMEM((n_heads, 1, s_len), F32),
                        per_head, per_head, per_head, per_head, per_head,
                        pltpu.VMEM((hp, n_chunks, hd, hd), BF16), pltpu.VMEM((hp, n_chunks, hd, hd), F32),
                        pltpu.VMEM((hp, s_len, hd), BF16), per_head,
                        pltpu.VMEM((hp, 3, CONV_PAD + s_len, hd), F32)],
        compiler_params=_cparams(2),
        name="gdn",
    )(proj, proj, proj, proj, gates, conv_w, conv_w, conv_w, alog_pad, dtb_pad, out_norm_w.reshape(1, hd))


def _gelu_tanh(x):
    return 0.5 * x * (1.0 + jnp.tanh(math.sqrt(2.0 / math.pi) * (x + 0.044715 * (x * x * x))))


def _lru_kernel(x_ref, y_ref, cw_ref, cb_ref, wa_ref, ba_ref, wx_ref, bx_ref, lam_ref, o_ref, pad_s):
    pad_s[:CONV_PAD, :] = jnp.zeros((CONV_PAD, pad_s.shape[1]), F32)
    xc = _causal_conv(x_ref[...].astype(F32), cw_ref[...], pad_s) + cb_ref[...]
    x16 = xc.astype(BF16)
    r = _sigmoid(_dot(x16, wa_ref[0]) + ba_ref[...])
    i = _sigmoid(_dot(x16, wx_ref[0]) + bx_ref[...])
    log_a = (-LRU_C) * r * _softplus(-lam_ref[...])
    a = jnp.exp(log_a)
    th = jnp.tanh(log_a)
    b = jnp.sqrt(-2.0 * th / (1.0 - th)) * i * xc
    sublanes = 8
    s_len = a.shape[0]
    pos = lax.broadcasted_iota(jnp.int32, a.shape, 0) % sublanes
    sh = 1
    while sh < sublanes:
        live = pos >= sh
        b = a * jnp.where(live, pltpu.roll(b, sh, 0), 0.0) + b
        a = a * jnp.where(live, pltpu.roll(a, sh, 0), 1.0)
        sh *= 2
    gate = _gelu_tanh(y_ref[...].astype(F32))
    carry = jnp.zeros((1, a.shape[1]), F32)
    pack = 2 * sublanes
    for t in range(s_len // pack):
        tiles = []
        for rows in (slice(t * pack, t * pack + sublanes), slice(t * pack + sublanes, (t + 1) * pack)):
            h = a[rows, :] * carry + b[rows, :]
            carry = h[sublanes - 1:, :]
            tiles.append(h * gate[rows, :])
        o_ref[t * pack:(t + 1) * pack, :] = jnp.concatenate(tiles, axis=0).astype(o_ref.dtype)


def _lru(proj, conv_w, conv_b, wa, ba, wx, bx, lam, *, batch, col_off):
    t = proj.shape[0]
    s_len = t // batch
    n_blocks, blk = wa.shape[0], wa.shape[1]
    seq = lambda off: pl.BlockSpec((s_len, blk), lambda b, n: (b, off + n))
    vec = pl.BlockSpec((1, blk), lambda b, n: (0, n))
    mat = pl.BlockSpec((1, blk, blk), lambda b, n: (n, 0, 0))
    width = n_blocks * blk
    return pl.pallas_call(
        _lru_kernel,
        grid=(batch, n_blocks),
        in_specs=[seq(col_off), seq(col_off + n_blocks),
                  pl.BlockSpec((CONV_WIDTH, blk), lambda b, n: (0, n)), vec, mat, vec, mat, vec, vec],
        out_specs=pl.BlockSpec((s_len, blk), lambda b, n: (b, n)),
        out_shape=jax.ShapeDtypeStruct((t, width), BF16),
        scratch_shapes=[pltpu.VMEM((CONV_PAD + s_len, blk), F32)],
        compiler_params=_cparams(2),
        name="rglru",
    )(proj, proj, conv_w, conv_b.reshape(1, width), wa, ba.reshape(1, width), wx, bx.reshape(1, width),
      lam.reshape(1, width))


LOG2E = math.log2(math.e)


def _softmax2_pv(logits2, v16):
    mx = jnp.max(logits2, axis=-1, keepdims=True)
    p = jnp.exp2(logits2 - mx)
    denom = jnp.sum(p, axis=-1, keepdims=True)
    return _dot(p.astype(BF16), v16) / denom


def _causal_tile(logits2, lo, extra=None):
    tq = logits2.shape[0]
    tri = lax.broadcasted_iota(jnp.int32, (tq, tq), 1) <= lax.broadcasted_iota(jnp.int32, (tq, tq), 0)
    parts = [] if lo == 0 else ([logits2[:, :lo]] if extra is None else extra)
    parts.append(jnp.where(tri, logits2[:, lo:], -jnp.inf))
    return parts[0] if len(parts) == 1 else jnp.concatenate(parts, axis=1)


def _fox_kernel(q_ref, k_ref, v_ref, g_ref, fb_ref, o_ref, cfc_s, cfr_s, *, n_heads):
    h = pl.program_id(1)
    s_len = q_ref.shape[0]
    tq = ATTN_Q_TILE

    @pl.when(h == 0)
    def _():
        z = g_ref[...] + fb_ref[...]
        log_f = -_softplus(-z)
        rows = _lane_cumsum(log_f.T[0:n_heads, :], s_len)
        for r in range(n_heads):
            cfr_s[r] = rows[r:r + 1, :]
        cfc_s[...] = _rows_to_columns(rows, LANES)

    cf_col2 = _lane_column(cfc_s[...], h) * LOG2E
    cf_row2 = cfr_s[h] * LOG2E
    k16 = k_ref[...]
    v16 = v_ref[...]
    scale2 = (HEAD_DIM ** -0.5) * LOG2E
    n_tiles = s_len // tq
    qk = lambda i: _dot_nt(q_ref[i * tq:(i + 1) * tq, :], k16[:(i + 1) * tq, :])
    qk_next = qk(0)
    for i in range(n_tiles):
        lo, hi = i * tq, (i + 1) * tq
        qk_cur, qk_next = qk_next, (qk(i + 1) if i + 1 < n_tiles else None)
        logits2 = qk_cur * scale2 + cf_col2[lo:hi, :] - cf_row2[:, :hi]
        o_ref[lo:hi, :] = _softmax2_pv(_causal_tile(logits2, lo), v16[:hi, :]).astype(o_ref.dtype)


def _fox(proj, gates, fb_pad, *, batch, n_heads):
    t = proj.shape[0]
    s_len = t // batch
    hd = HEAD_DIM
    seq = lambda off: pl.BlockSpec((s_len, hd), lambda b, h: (b, off + h))
    return pl.pallas_call(
        functools.partial(_fox_kernel, n_heads=n_heads),
        grid=(batch, n_heads),
        in_specs=[seq(0), seq(n_heads), seq(2 * n_heads),
                  pl.BlockSpec((s_len, LANES), lambda b, h: (b, 0)),
                  pl.BlockSpec((1, LANES), lambda b, h: (0, 0))],
        out_specs=pl.BlockSpec((s_len, hd), lambda b, h: (b, h)),
        out_shape=jax.ShapeDtypeStruct((t, n_heads * hd), BF16),
        scratch_shapes=[pltpu.VMEM((s_len, LANES), F32), pltpu.VMEM((n_heads, 1, s_len), F32)],
        compiler_params=_cparams(2),
        name="fox",
    )(proj, proj, proj, gates, fb_pad)


def _rotary(x, cos_t, sin_lo, sin_hi):
    half = ROPE_DIM // 2
    return x * cos_t + pltpu.roll(x, LANES - half, 1) * sin_lo + pltpu.roll(x, half, 1) * sin_hi


def _moba_kernel(q_ref, k_ref, v_ref, cos_ref, slo_ref, shi_ref, o_ref, q_s, sel_s):
    s_len = q_ref.shape[0]
    blk = MOBA_BLOCK
    nb = s_len // blk
    cos_t, sin_lo, sin_hi = cos_ref[...], slo_ref[...], shi_ref[...]
    q = _rotary(q_ref[...].astype(F32), cos_t, sin_lo, sin_hi)
    k = _rotary(k_ref[...].astype(F32), cos_t, sin_lo, sin_hi)
    q_s[...] = q.astype(BF16)
    k16 = k.astype(BF16)
    v16 = v_ref[...]

    nb_pad = -(-nb // 8) * 8
    k_mean = jnp.concatenate([jnp.mean(k[n * blk:(n + 1) * blk, :], axis=0, keepdims=True) for n in range(nb)]
                             + ([jnp.zeros((nb_pad - nb, HEAD_DIM), F32)] if nb_pad > nb else []), axis=0)
    q_hi = q_s[...]
    q_lo = (q - q_hi.astype(F32)).astype(BF16)
    km_hi = k_mean.astype(BF16)
    km_lo = (k_mean - km_hi.astype(F32)).astype(BF16)
    gate = (_dot_nt(km_hi, q_hi) + _dot_nt(km_hi, q_lo)) + (_dot_nt(km_lo, q_hi) + _dot_nt(km_lo, q_lo))
    blk_row = lax.broadcasted_iota(jnp.int32, gate.shape, 0)
    q_blk = lax.broadcasted_iota(jnp.int32, gate.shape, 1) // blk
    past = blk_row < q_blk
    gate = jnp.where(past, gate, -jnp.inf)
    rank = jnp.zeros(gate.shape, F32)
    for m in range(nb - 1):
        g_m = gate[m:m + 1, :]
        ahead = (g_m > gate) | ((g_m == gate) & (blk_row > m))
        rank = rank + jnp.where(ahead & (q_blk > m), 1.0, 0.0)
    keep_bias = jnp.where(past & (rank < MOBA_TOPK), 0.0, -jnp.inf)
    sel_s[...] = _rows_to_columns(keep_bias, LANES)

    scale2 = (HEAD_DIM ** -0.5) * LOG2E
    for i in range(nb):
        lo, hi = i * blk, (i + 1) * blk
        logits2 = _dot_nt(q_s[lo:hi, :], k16[:hi, :]) * scale2
        sel = sel_s[lo:hi, :]
        extra = [logits2[:, j * blk:(j + 1) * blk] + sel[:, j:j + 1] for j in range(i)]
        o_ref[lo:hi, :] = _softmax2_pv(_causal_tile(logits2, lo, extra), v16[:hi, :]).astype(o_ref.dtype)


def _rope_tables(s_len):
    half = ROPE_DIM // 2
    inv_freq = ROPE_THETA ** (-jnp.arange(half, dtype=F32) / half)
    ang = jnp.arange(s_len, dtype=F32)[:, None] * inv_freq[None, :]
    cos, sin = jnp.cos(ang), jnp.sin(ang)
    zeros = jnp.zeros((s_len, HEAD_DIM - ROPE_DIM), F32)
    z_half = jnp.zeros((s_len, half), F32)
    cos_t = jnp.concatenate([cos, cos, zeros + 1.0], axis=1)
    sin_lo = jnp.concatenate([-sin, z_half, zeros], axis=1)
    sin_hi = jnp.concatenate([z_half, sin, zeros], axis=1)
    return cos_t, sin_lo, sin_hi


def _moba(proj, *, batch, n_heads, col_off):
    t = proj.shape[0]
    s_len = t // batch
    hd = HEAD_DIM
    seq = lambda off: pl.BlockSpec((s_len, hd), lambda b, h: (b, off + h))
    tab = pl.BlockSpec((s_len, hd), lambda b, h: (0, 0))
    return pl.pallas_call(
        _moba_kernel,
        grid=(batch, n_heads),
        in_specs=[seq(col_off), seq(col_off + n_heads), seq(col_off + 2 * n_heads), tab, tab, tab],
        out_specs=pl.BlockSpec((s_len, hd), lambda b, h: (b, h)),
        out_shape=jax.ShapeDtypeStruct((t, n_heads * hd), BF16),
        scratch_shapes=[pltpu.VMEM((s_len, hd), BF16), pltpu.VMEM((s_len, LANES), F32)],
        compiler_params=_cparams(2),
        name="moba",
    )(proj, proj, proj, *_rope_tables(s_len))


def _pad_lanes(parts):
    row = jnp.concatenate([p.reshape(1, -1).astype(F32) for p in parts], axis=1)
    return jnp.pad(row, ((0, 0), (0, LANES - row.shape[1])))


def _regroup_kernel(w_ref, main_ref, gate_ref, *, main_runs, gate_runs):
    w = w_ref[...]
    main_ref[...] = jnp.concatenate([w[:, a:b] for a, b in main_runs], axis=1).astype(main_ref.dtype)
    gate = [w[:, a:b] for a, b in gate_runs]
    n_gate = sum(b - a for a, b in gate_runs)
    gate.append(jnp.zeros((w.shape[0], gate_ref.shape[1] - n_gate), w.dtype))
    gate_ref[...] = jnp.concatenate(gate, axis=1).astype(gate_ref.dtype)


def _split_w_in(w, layer, sizes, main_idx, gate_idx, *, tr=256):
    offs = [0]
    for n in sizes:
        offs.append(offs[-1] + n)

    def runs(idx):
        out = []
        for i in idx:
            if out and out[-1][1] == offs[i]:
                out[-1] = (out[-1][0], offs[i + 1])
            else:
                out.append((offs[i], offs[i + 1]))
        return tuple(out)

    _, d, n_in = w.shape
    n_main = sum(sizes[i] for i in main_idx)
    return pl.pallas_call(
        functools.partial(_regroup_kernel, main_runs=runs(main_idx), gate_runs=runs(gate_idx)),
        grid=(d // tr,),
        in_specs=[pl.BlockSpec((None, tr, n_in), lambda i: (layer, i, 0))],
        out_specs=[pl.BlockSpec((tr, n_main), lambda i: (i, 0)), pl.BlockSpec((tr, LANES), lambda i: (i, 0))],
        out_shape=[jax.ShapeDtypeStruct((d, n_main), BF16), jax.ShapeDtypeStruct((d, LANES), BF16)],
        compiler_params=_cparams(1),
        name="regroup_w_in",
    )(w)


def kernel(x, ab_norm, ab_w_in, ab_conv_qkv, ab_a_log, ab_dt_bias, ab_out_norm, ab_lru_conv_w, ab_lru_conv_b,
           ab_lru_wa, ab_lru_ba, ab_lru_wx, ab_lru_bx, ab_lru_lambda, ab_w_out, cd_norm, cd_w_in, cd_f_bias,
           cd_w_out, ffn_norm, ffn_w_gate, ffn_w_up, ffn_w_down, final_norm):
    batch, s_len, d_model = x.shape
    depth = ffn_norm.shape[0]
    gdn_heads = ab_a_log.shape[1]
    gdn_width = gdn_heads * HEAD_DIM
    lru_width = ab_lru_lambda.shape[1]
    fox_heads = cd_f_bias.shape[1]
    fox_width = fox_heads * HEAD_DIM
    moba_width = d_model - fox_width
    moba_heads = moba_width // HEAD_DIM
    ab_sizes = (gdn_width,) * 4 + (gdn_heads,) * 2 + (lru_width,) * 2
    cd_sizes = (fox_width,) * 3 + (fox_heads,) + (moba_width,) * 3

    w_gate16, w_up16, w_down16 = (w.astype(BF16) for w in (ffn_w_gate, ffn_w_up, ffn_w_down))
    h = x.reshape(batch * s_len, d_model)
    for layer in range(depth):
        j = layer // 2
        if layer % 2 == 0:
            w_main, w_gate = _split_w_in(ab_w_in, j, ab_sizes, (0, 1, 2, 3, 6, 7), (4, 5))
            proj, gates = _inproj(h, ab_norm[j], w_main, w_gate)
            zeros_h = jnp.zeros((gdn_heads,), F32)
            o_a = _gdn(proj, gates, ab_conv_qkv[j], _pad_lanes([zeros_h, ab_a_log[j]]),
                       _pad_lanes([zeros_h, ab_dt_bias[j]]), ab_out_norm[j], batch=batch, n_heads=gdn_heads)
            o_b = _lru(proj, ab_lru_conv_w[j], ab_lru_conv_b[j], ab_lru_wa[j].astype(BF16), ab_lru_ba[j],
                       ab_lru_wx[j].astype(BF16), ab_lru_bx[j], ab_lru_lambda[j], batch=batch,
                       col_off=4 * gdn_heads)
            h = _outproj(h, o_a, o_b, ab_w_out[j].astype(BF16))
        else:
            w_main, w_gate = _split_w_in(cd_w_in, j, cd_sizes, (0, 1, 2, 4, 5, 6), (3,))
            proj, gates = _inproj(h, cd_norm[j], w_main, w_gate)
            o_c = _fox(proj, gates, _pad_lanes([cd_f_bias[j]]), batch=batch, n_heads=fox_heads)
            o_d = _moba(proj, batch=batch, n_heads=moba_heads, col_off=3 * fox_heads)
            h = _outproj(h, o_c, o_d, cd_w_out[j].astype(BF16))
        h = _ffn(h, ffn_norm[layer], w_gate16, w_up16, w_down16, final_norm, layer=layer,
                 final_norm=(layer == depth - 1))
    return h.reshape(batch, s_len, d_model)
```

```python
import functools
import math

import jax
import jax.numpy as jnp
from jax import lax
from jax.experimental import pallas as pl
from jax.experimental.pallas import tpu as pltpu

F32 = jnp.float32
BF16 = jnp.bfloat16

HEAD_DIM = 128
LANES = 128
NORM_EPS = 1e-6
CONV_WIDTH = 4
GDN_CHUNK = 64
GDN_GROUP = 8
GDN_HEADS_PER_STEP = 2
GDN_STAGES = 10
LRU_C = 8.0
MOBA_BLOCK = 256
MOBA_TOPK = 3
ROPE_THETA = 500000.0
ROPE_DIM = HEAD_DIM // 4
ATTN_Q_TILE = 256
VMEM_LIMIT_BYTES = 48 * 1024 * 1024
FFN_VMEM_LIMIT_BYTES = 58 * 1024 * 1024


def _cparams(n_axes, vmem_limit_bytes=VMEM_LIMIT_BYTES):
    return pltpu.CompilerParams(dimension_semantics=("arbitrary",) * n_axes,
                                vmem_limit_bytes=vmem_limit_bytes)


def _dot(a, b):
    return jnp.dot(a, b, preferred_element_type=F32)


def _dot_nt(a, b):
    return lax.dot_general(a, b, (((1,), (1,)), ((), ())), preferred_element_type=F32)


def _dot_tn(a, b):
    return lax.dot_general(a, b, (((0,), (0,)), ((), ())), preferred_element_type=F32)


def _rmsnorm(x, w):
    ms = jnp.mean(x * x, axis=-1, keepdims=True)
    return x * lax.rsqrt(ms + NORM_EPS) * w


def _sigmoid(x):
    return 0.5 * jnp.tanh(0.5 * x) + 0.5


def _softplus(x):
    return jnp.maximum(x, 0.0) + jnp.log1p(jnp.exp(-jnp.abs(x)))


def _silu(x):
    half = 0.5 * x
    return half + half * jnp.tanh(half)


CONV_PAD = 8


def _causal_conv(x, w, pad_ref, r0=0):
    rows = x.shape[0]
    k = w.shape[0]
    base = CONV_PAD + r0
    pad_ref[base:base + rows, :] = x
    acc = x * w[k - 1:k, :]
    for j in range(k - 1):
        acc = acc + pad_ref[base - (k - 1 - j):base - (k - 1 - j) + rows, :] * w[j:j + 1, :]
    return acc


def _lane_column(x, idx):
    lane = lax.broadcasted_iota(jnp.int32, x.shape, 1)
    return jnp.sum(jnp.where(lane == idx, x, 0.0), axis=-1, keepdims=True)


def _lane_cumsum(x, period):
    pos = lax.broadcasted_iota(jnp.int32, x.shape, 1) % period
    sh = 1
    while sh < period:
        x = x + jnp.where(pos >= sh, pltpu.roll(x, sh, 1), 0.0)
        sh *= 2
    return x


def _rows_to_columns(rows8, n_lanes):
    pad = jnp.zeros((n_lanes - rows8.shape[0], rows8.shape[1]), F32)
    return jnp.concatenate([rows8, pad], axis=0).T


def _inproj_kernel(x_ref, nw_ref, w_ref, wg_ref, o_ref, og_ref, xn_ref):
    @pl.when(pl.program_id(1) == 0)
    def _():
        xn = _rmsnorm(x_ref[...], nw_ref[...]).astype(BF16)
        xn_ref[...] = xn
        og_ref[...] = _dot(xn, wg_ref[...])

    o_ref[...] = _dot(xn_ref[...], w_ref[...]).astype(o_ref.dtype)


def _inproj(x, nw, w, wg, *, tm=1024, tn=1024):
    t, d = x.shape
    n = w.shape[1]
    return pl.pallas_call(
        _inproj_kernel,
        grid=(t // tm, n // tn),
        in_specs=[pl.BlockSpec((tm, d), lambda i, j: (i, 0)),
                  pl.BlockSpec((1, d), lambda i, j: (0, 0)),
                  pl.BlockSpec((d, tn), lambda i, j: (0, j)),
                  pl.BlockSpec((d, LANES), lambda i, j: (0, 0))],
        out_specs=[pl.BlockSpec((tm, tn), lambda i, j: (i, j)),
                   pl.BlockSpec((tm, LANES), lambda i, j: (i, 0))],
        out_shape=[jax.ShapeDtypeStruct((t, n), BF16), jax.ShapeDtypeStruct((t, LANES), F32)],
        scratch_shapes=[pltpu.VMEM((tm, d), BF16)],
        compiler_params=_cparams(2),
        name="inproj",
    )(x, nw.reshape(1, d), w, wg)


def _outproj_kernel(res_ref, a1_ref, a2_ref, w1_ref, w2_ref, o_ref):
    o_ref[...] = res_ref[...] + _dot(a1_ref[...], w1_ref[...]) + _dot(a2_ref[...], w2_ref[...])


def _outproj(res, a1, a2, w, *, tm=512, tn=2048):
    t, d = res.shape
    k1, k2 = a1.shape[1], a2.shape[1]
    assert k1 == k2 and w.shape == (k1 + k2, d)
    return pl.pallas_call(
        _outproj_kernel,
        grid=(t // tm, d // tn),
        in_specs=[pl.BlockSpec((tm, tn), lambda i, j: (i, j)),
                  pl.BlockSpec((tm, k1), lambda i, j: (i, 0)),
                  pl.BlockSpec((tm, k2), lambda i, j: (i, 0)),
                  pl.BlockSpec((k1, tn), lambda i, j: (0, j)),
                  pl.BlockSpec((k2, tn), lambda i, j: (1, j))],
        out_specs=pl.BlockSpec((tm, tn), lambda i, j: (i, j)),
        out_shape=jax.ShapeDtypeStruct((t, d), F32),
        compiler_params=_cparams(2),
        name="outproj",
    )(res, a1, a2, w, w)


def _ffn_kernel(x_ref, nw_ref, wg_ref, wu_ref, wd_ref, fnw_ref, o_ref, xn_ref, *, final_norm, down_chunk):
    f = pl.program_id(1)

    @pl.when(f == 0)
    def _():
        x = x_ref[...]
        xn_ref[...] = _rmsnorm(x, nw_ref[...]).astype(BF16)
        o_ref[...] = x

    xn = xn_ref[...]
    act = (_silu(_dot(xn, wg_ref[...])) * _dot(xn, wu_ref[...])).astype(BF16)
    for c0 in range(0, o_ref.shape[1], down_chunk):
        o_ref[:, c0:c0 + down_chunk] += _dot(act, wd_ref[:, c0:c0 + down_chunk])

    if final_norm:
        @pl.when(f == pl.num_programs(1) - 1)
        def _():
            o_ref[...] = _rmsnorm(o_ref[...], fnw_ref[...])


def _ffn(x, nw, wg, wu, wd, fnw, *, layer, final_norm, tm=1024, tf=512, down_chunk=512):
    t, d = x.shape
    dff = wg.shape[2]
    return pl.pallas_call(
        functools.partial(_ffn_kernel, final_norm=final_norm, down_chunk=down_chunk),
        grid=(t // tm, dff // tf),
        in_specs=[pl.BlockSpec((tm, d), lambda i, f: (i, 0)),
                  pl.BlockSpec((1, d), lambda i, f: (0, 0)),
                  pl.BlockSpec((None, d, tf), lambda i, f: (layer, 0, f)),
                  pl.BlockSpec((None, d, tf), lambda i, f: (layer, 0, f)),
                  pl.BlockSpec((None, tf, d), lambda i, f: (layer, f, 0)),
                  pl.BlockSpec((1, d), lambda i, f: (0, 0))],
        out_specs=pl.BlockSpec((tm, d), lambda i, f: (i, 0)),
        out_shape=jax.ShapeDtypeStruct((t, d), F32),
        scratch_shapes=[pltpu.VMEM((tm, d), BF16)],
        compiler_params=_cparams(2, FFN_VMEM_LIMIT_BYTES),
        name="ffn",
    )(x, nw.reshape(1, d), wg, wu, wd, fnw.reshape(1, d))


def _unit_lower_inverses(ms):
    c = ms[0].shape[0]
    eye = (lax.broadcasted_iota(jnp.int32, (c, c), 0) == lax.broadcasted_iota(jnp.int32, (c, c), 1)).astype(F32)
    invs = [eye - m for m in ms]
    m16s = [m.astype(BF16) for m in ms]
    ps = [_dot(m16, m16) for m16 in m16s]
    yield
    span = 2
    while span < c:
        p16s = [p.astype(BF16) for p in ps]
        if span * 2 < c:
            boths = [_dot(jnp.concatenate([inv.astype(BF16), p16], axis=0), p16) for inv, p16 in zip(invs, p16s)]
            invs = [inv + both[:c] for inv, both in zip(invs, boths)]
            ps = [both[c:] for both in boths]
        else:
            invs = [inv + _dot(inv.astype(BF16), p16) for inv, p16 in zip(invs, p16s)]
        span *= 2
        yield
    return invs


def _gdn_kernel(q_ref, k_ref, v_ref, z_ref, g_ref, cq_ref, ck_ref, cv_ref, alog_ref, dtb_ref, onw_ref,
                o_ref, beta_s, gcc_s, gcr_s, q_s, k_s, v_s, bc_s, gc_s, p_s, b_s, qp_s, out_s, pad_s,
                *, n_heads, hp):
    hb = pl.program_id(1)
    s_len = q_ref.shape[0]
    c = GDN_CHUNK
    group = GDN_GROUP
    hd = HEAD_DIM

    @pl.when(hb == 0)
    def _():
        logits = g_ref[...]
        beta_s[...] = _sigmoid(logits)
        log_decay = -jnp.exp(alog_ref[...]) * _softplus(logits + dtb_ref[...])
        rows = log_decay.T[n_heads:2 * n_heads, :]
        rows = _lane_cumsum(rows, c)
        for r in range(n_heads):
            gcr_s[r] = rows[r:r + 1, :]
        gcc_s[...] = _rows_to_columns(rows, LANES)

    span = group * c
    n_groups = s_len // span
    streams = ((q_ref, cq_ref, q_s), (k_ref, ck_ref, k_s), (v_ref, cv_ref, v_s))
    for j in range(hp):
        for a in range(len(streams)):
            pad_s[j, a, :CONV_PAD, :] = jnp.zeros((CONV_PAD, hd), F32)

    def input_thunks(gi):
        rows = slice(gi * span, (gi + 1) * span)
        thunks = []
        for j in range(hp):
            cols = slice(j * hd, (j + 1) * hd)
            head = hb * hp + j

            def gate_columns(j=j, head=head):
                bc_s[j, rows, :] = jnp.broadcast_to(_lane_column(beta_s[rows, :], head), (span, hd))
                gc_s[j, rows, :] = jnp.broadcast_to(_lane_column(gcc_s[rows, :], head), (span, hd))

            thunks.append(gate_columns)
            for a, (src, cw_ref, dst) in enumerate(streams):
                def conv_stream(j=j, a=a, cols=cols, src=src, cw_ref=cw_ref, dst=dst):
                    x = src[rows, cols].astype(F32)
                    y = _silu(_causal_conv(x, cw_ref[:, cols], pad_s.at[j, a], gi * span))
                    if dst is q_s:
                        y = y * (lax.rsqrt(jnp.sum(y * y, axis=-1, keepdims=True) + NORM_EPS) * (hd ** -0.5))
                    elif dst is k_s:
                        y = y * lax.rsqrt(jnp.sum(y * y, axis=-1, keepdims=True) + NORM_EPS)
                    dst[j, rows, :] = y

                thunks.append(conv_stream)
        return thunks

    ri = lax.broadcasted_iota(jnp.int32, (c, c), 0)
    ci = lax.broadcasted_iota(jnp.int32, (c, c), 1)
    incl = ri >= ci
    strict = ri > ci

    def chunk_terms(gi):
        gc_rows = [gcr_s[hb * hp + j, :, gi * span:(gi + 1) * span] for j in range(hp)]
        units = [(j, g) for g in range(group) for j in range(hp)]
        rows_u, scores_u, decay_u, rhs_u, kdec_u, qdec_u = [], [], [], [], [], []
        for j, g in units:
            rows = slice((gi * group + g) * c, (gi * group + g + 1) * c)
            qc, kc, vc = q_s[j, rows, :], k_s[j, rows, :], v_s[j, rows, :]
            beta, gc = bc_s[j, rows, :], gc_s[j, rows, :]
            gc_row = gc_rows[j][:, g * c:(g + 1) * c]
            decay_u.append(jnp.where(incl, jnp.exp(jnp.where(incl, gc[:, :c] - gc_row, 0.0)), 0.0))
            k_beta = kc * beta
            e_gc = jnp.exp(gc)
            lhs = jnp.concatenate([k_beta.astype(BF16), qc.astype(BF16)], axis=0)
            scores_u.append(_dot_nt(lhs, kc.astype(BF16)))
            rhs_u.append(jnp.concatenate([(vc * beta).astype(BF16), (k_beta * e_gc).astype(BF16)], axis=1))
            kdec_u.append((kc * jnp.exp(gc[c - 1:c, :] - gc)).astype(BF16))
            qdec_u.append(qc * e_gc)
            rows_u.append(rows)
        yield
        ms = [jnp.where(strict, sc[:c] * dec, 0.0) for sc, dec in zip(scores_u, decay_u)]
        attn_u = [jnp.where(incl, sc[c:] * dec, 0.0).astype(BF16) for sc, dec in zip(scores_u, decay_u)]
        t_u = yield from _unit_lower_inverses(ms)
        uw_u = [_dot(t.astype(BF16), rhs).astype(BF16) for t, rhs in zip(t_u, rhs_u)]
        yield
        bp_u = [_dot_tn(kdec, uw) for kdec, uw in zip(kdec_u, uw_u)]
        yield
        aw_u = [_dot(attn, uw) for attn, uw in zip(attn_u, uw_u)]
        for (j, g), rows, bp, aw, qdec in zip(units, rows_u, bp_u, aw_u, qdec_u):
            n = gi * group + g
            b_s[j, n] = bp[:, :hd]
            p_s[j, n] = bp[:, hd:].astype(BF16)
            qp_s[j, rows, :] = (qdec - aw[:, hd:]).astype(BF16)
            out_s[j, rows, :] = aw[:, :hd]
        yield

    def recurrence_steps(gi, states):
        def step(n):
            rows = slice(n * c, (n + 1) * c)
            for j in range(hp):
                s16 = states[j].astype(BF16)
                out_s[j, rows, :] += _dot(qp_s[j, rows, :], s16)
                g_last = gc_s[j, (n + 1) * c - 1:(n + 1) * c, :]
                states[j] = states[j] * jnp.exp(g_last) - _dot(p_s[j, n], s16) + b_s[j, n]
        return [functools.partial(step, gi * group + g) for g in range(group)]

    def output_thunks(gi):
        rows = slice(gi * span, (gi + 1) * span)

        def finish(j):
            cols = slice(j * hd, (j + 1) * hd)
            o = _rmsnorm(out_s[j, rows, :], onw_ref[...]) * _silu(z_ref[rows, cols].astype(F32))
            o_ref[rows, cols] = o.astype(o_ref.dtype)
        return [functools.partial(finish, j) for j in range(hp)]

    def interleave(stages, side):
        side = list(side)
        per_stage = -(-len(side) // GDN_STAGES)
        for _ in stages:
            for _ in range(min(per_stage, len(side))):
                side.pop(0)()
        for thunk in side:
            thunk()

    states = [jnp.zeros((hd, hd), F32) for _ in range(hp)]
    for thunk in input_thunks(0):
        thunk()
    for gi in range(n_groups):
        side = []
        rec = recurrence_steps(gi - 1, states) if gi > 0 else []
        prep = input_thunks(gi + 1) if gi + 1 < n_groups else []
        done = output_thunks(gi - 2) if gi > 1 else []
        while rec or prep or done:
            for lst in (rec, prep, done):
                if lst:
                    side.append(lst.pop(0))
        interleave(chunk_terms(gi), side)
    for thunk in recurrence_steps(n_groups - 1, states):
        thunk()
    for gi in range(max(n_groups - 2, 0), n_groups):
        for thunk in output_thunks(gi):
            thunk()


def _gdn(proj, gates, conv_w, alog_pad, dtb_pad, out_norm_w, *, batch, n_heads, hp=GDN_HEADS_PER_STEP):
    t = proj.shape[0]
    s_len = t // batch
    hd = HEAD_DIM
    nblk = n_heads // hp
    n_chunks = s_len // GDN_CHUNK
    assert n_heads % hp == 0 and s_len % (GDN_GROUP * GDN_CHUNK) == 0
    seq = lambda off: pl.BlockSpec((s_len, hp * hd), lambda b, h: (b, off + h))
    cw = lambda off: pl.BlockSpec((CONV_WIDTH, hp * hd), lambda b, h: (0, off + h))
    row = pl.BlockSpec((1, LANES), lambda b, h: (0, 0))
    act = pltpu.VMEM((s_len, LANES), F32)
    per_head = pltpu.VMEM((hp, s_len, hd), F32)
    return pl.pallas_call(
        functools.partial(_gdn_kernel, n_heads=n_heads, hp=hp),
        grid=(batch, nblk),
        in_specs=[seq(0), seq(nblk), seq(2 * nblk), seq(3 * nblk),
                  pl.BlockSpec((s_len, LANES), lambda b, h: (b, 0)),
                  cw(0), cw(nblk), cw(2 * nblk), row, row, row],
        out_specs=pl.BlockSpec((s_len, hp * hd), lambda b, h: (b, h)),
        out_shape=jax.ShapeDtypeStruct((t, n_heads * hd), BF16),
        scratch_shapes=[act, act, pltpu.VMEM((n_heads, 1, s_len), F32),
                        per_head, per_head, per_head, per_head, per_head,
                        pltpu.VMEM((hp, n_chunks, hd, hd), BF16), pltpu.VMEM((hp, n_chunks, hd, hd), F32),
                        pltpu.VMEM((hp, s_len, hd), BF16), per_head,
                        pltpu.VMEM((hp, 3, CONV_PAD + s_len, hd), F32)],
        compiler_params=_cparams(2),
        name="gdn",
    )(proj, proj, proj, proj, gates, conv_w, conv_w, conv_w, alog_pad, dtb_pad, out_norm_w.reshape(1, hd))


def _gelu_tanh(x):
    return 0.5 * x * (1.0 + jnp.tanh(math.sqrt(2.0 / math.pi) * (x + 0.044715 * (x * x * x))))


def _lru_kernel(x_ref, y_ref, cw_ref, cb_ref, wa_ref, ba_ref, wx_ref, bx_ref, lam_ref, o_ref, pad_s):
    pad_s[:CONV_PAD, :] = jnp.zeros((CONV_PAD, pad_s.shape[1]), F32)
    xc = _causal_conv(x_ref[...].astype(F32), cw_ref[...], pad_s) + cb_ref[...]
    x16 = xc.astype(BF16)
    r = _sigmoid(_dot(x16, wa_ref[0]) + ba_ref[...])
    i = _sigmoid(_dot(x16, wx_ref[0]) + bx_ref[...])
    log_a = (-LRU_C) * r * _softplus(-lam_ref[...])
    a = jnp.exp(log_a)
    th = jnp.tanh(log_a)
    b = jnp.sqrt(-2.0 * th / (1.0 - th)) * i * xc
    sublanes = 8
    s_len = a.shape[0]
    pos = lax.broadcasted_iota(jnp.int32, a.shape, 0) % sublanes
    sh = 1
    while sh < sublanes:
        live = pos >= sh
        b = a * jnp.where(live, pltpu.roll(b, sh, 0), 0.0) + b
        a = a * jnp.where(live, pltpu.roll(a, sh, 0), 1.0)
        sh *= 2
    gate = _gelu_tanh(y_ref[...].astype(F32))
    carry = jnp.zeros((1, a.shape[1]), F32)
    pack = 2 * sublanes
    for t in range(s_len // pack):
        tiles = []
        for rows in (slice(t * pack, t * pack + sublanes), slice(t * pack + sublanes, (t + 1) * pack)):
            h = a[rows, :] * carry + b[rows, :]
            carry = h[sublanes - 1:, :]
            tiles.append(h * gate[rows, :])
        o_ref[t * pack:(t + 1) * pack, :] = jnp.concatenate(tiles, axis=0).astype(o_ref.dtype)


def _lru(proj, conv_w, conv_b, wa, ba, wx, bx, lam, *, batch, col_off):
    t = proj.shape[0]
    s_len = t // batch
    n_blocks, blk = wa.shape[0], wa.shape[1]
    seq = lambda off: pl.BlockSpec((s_len, blk), lambda b, n: (b, off + n))
    vec = pl.BlockSpec((1, blk), lambda b, n: (0, n))
    mat = pl.BlockSpec((1, blk, blk), lambda b, n: (n, 0, 0))
    width = n_blocks * blk
    return pl.pallas_call(
        _lru_kernel,
        grid=(batch, n_blocks),
        in_specs=[seq(col_off), seq(col_off + n_blocks),
                  pl.BlockSpec((CONV_WIDTH, blk), lambda b, n: (0, n)), vec, mat, vec, mat, vec, vec],
        out_specs=pl.BlockSpec((s_len, blk), lambda b, n: (b, n)),
        out_shape=jax.ShapeDtypeStruct((t, width), BF16),
        scratch_shapes=[pltpu.VMEM((CONV_PAD + s_len, blk), F32)],
        compiler_params=_cparams(2),
        name="rglru",
    )(proj, proj, conv_w, conv_b.reshape(1, width), wa, ba.reshape(1, width), wx, bx.reshape(1, width),
      lam.reshape(1, width))


LOG2E = math.log2(math.e)


def _softmax2_pv(logits2, v16):
    mx = jnp.max(logits2, axis=-1, keepdims=True)
    p = jnp.exp2(logits2 - mx)
    denom = jnp.sum(p, axis=-1, keepdims=True)
    return _dot(p.astype(BF16), v16) / denom


def _causal_tile(logits2, lo, extra=None):
    tq = logits2.shape[0]
    tri = lax.broadcasted_iota(jnp.int32, (tq, tq), 1) <= lax.broadcasted_iota(jnp.int32, (tq, tq), 0)
    parts = [] if lo == 0 else ([logits2[:, :lo]] if extra is None else extra)
    parts.append(jnp.where(tri, logits2[:, lo:], -jnp.inf))
    return parts[0] if len(parts) == 1 else jnp.concatenate(parts, axis=1)


def _fox_kernel(q_ref, k_ref, v_ref, g_ref, fb_ref, o_ref, cfc_s, cfr_s, *, n_heads):
    h = pl.program_id(1)
    s_len = q_ref.shape[0]
    tq = ATTN_Q_TILE

    @pl.when(h == 0)
    def _():
        z = g_ref[...] + fb_ref[...]
        log_f = -_softplus(-z)
        rows = _lane_cumsum(log_f.T[0:n_heads, :], s_len)
        for r in range(n_heads):
            cfr_s[r] = rows[r:r + 1, :]
        cfc_s[...] = _rows_to_columns(rows, LANES)

    cf_col2 = _lane_column(cfc_s[...], h) * LOG2E
    cf_row2 = cfr_s[h] * LOG2E
    k16 = k_ref[...]
    v16 = v_ref[...]
    scale2 = (HEAD_DIM ** -0.5) * LOG2E
    n_tiles = s_len // tq
    qk = lambda i: _dot_nt(q_ref[i * tq:(i + 1) * tq, :], k16[:(i + 1) * tq, :])
    qk_next = qk(0)
    for i in range(n_tiles):
        lo, hi = i * tq, (i + 1) * tq
        qk_cur, qk_next = qk_next, (qk(i + 1) if i + 1 < n_tiles else None)
        logits2 = qk_cur * scale2 + cf_col2[lo:hi, :] - cf_row2[:, :hi]
        o_ref[lo:hi, :] = _softmax2_pv(_causal_tile(logits2, lo), v16[:hi, :]).astype(o_ref.dtype)


def _fox(proj, gates, fb_pad, *, batch, n_heads):
    t = proj.shape[0]
    s_len = t // batch
    hd = HEAD_DIM
    seq = lambda off: pl.BlockSpec((s_len, hd), lambda b, h: (b, off + h))
    return pl.pallas_call(
        functools.partial(_fox_kernel, n_heads=n_heads),
        grid=(batch, n_heads),
        in_specs=[seq(0), seq(n_heads), seq(2 * n_heads),
                  pl.BlockSpec((s_len, LANES), lambda b, h: (b, 0)),
                  pl.BlockSpec((1, LANES), lambda b, h: (0, 0))],
        out_specs=pl.BlockSpec((s_len, hd), lambda b, h: (b, h)),
        out_shape=jax.ShapeDtypeStruct((t, n_heads * hd), BF16),
        scratch_shapes=[pltpu.VMEM((s_len, LANES), F32), pltpu.VMEM((n_heads, 1, s_len), F32)],
        compiler_params=_cparams(2),
        name="fox",
    )(proj, proj, proj, gates, fb_pad)


def _rotary(x, cos_t, sin_lo, sin_hi):
    half = ROPE_DIM // 2
    return x * cos_t + pltpu.roll(x, LANES - half, 1) * sin_lo + pltpu.roll(x, half, 1) * sin_hi


def _moba_kernel(q_ref, k_ref, v_ref, cos_ref, slo_ref, shi_ref, o_ref, q_s, sel_s):
    s_len = q_ref.shape[0]
    blk = MOBA_BLOCK
    nb = s_len // blk
    cos_t, sin_lo, sin_hi = cos_ref[...], slo_ref[...], shi_ref[...]
    q = _rotary(q_ref[...].astype(F32), cos_t, sin_lo, sin_hi)
    k = _rotary(k_ref[...].astype(F32), cos_t, sin_lo, sin_hi)
    q_s[...] = q.astype(BF16)
    k16 = k.astype(BF16)
    v16 = v_ref[...]

    nb_pad = -(-nb // 8) * 8
    k_mean = jnp.concatenate([jnp.mean(k[n * blk:(n + 1) * blk, :], axis=0, keepdims=True) for n in range(nb)]
                             + ([jnp.zeros((nb_pad - nb, HEAD_DIM), F32)] if nb_pad > nb else []), axis=0)
    q_hi = q_s[...]
    q_lo = (q - q_hi.astype(F32)).astype(BF16)
    km_hi = k_mean.astype(BF16)
    km_lo = (k_mean - km_hi.astype(F32)).astype(BF16)
    gate = (_dot_nt(km_hi, q_hi) + _dot_nt(km_hi, q_lo)) + (_dot_nt(km_lo, q_hi) + _dot_nt(km_lo, q_lo))
    blk_row = lax.broadcasted_iota(jnp.int32, gate.shape, 0)
    q_blk = lax.broadcasted_iota(jnp.int32, gate.shape, 1) // blk
    past = blk_row < q_blk
    gate = jnp.where(past, gate, -jnp.inf)
    rank = jnp.zeros(gate.shape, F32)
    for m in range(nb - 1):
        g_m = gate[m:m + 1, :]
        ahead = (g_m > gate) | ((g_m == gate) & (blk_row > m))
        rank = rank + jnp.where(ahead & (q_blk > m), 1.0, 0.0)
    keep_bias = jnp.where(past & (rank < MOBA_TOPK), 0.0, -jnp.inf)
    sel_s[...] = _rows_to_columns(keep_bias, LANES)

    scale2 = (HEAD_DIM ** -0.5) * LOG2E
    qk = lambda i: _dot_nt(q_s[i * blk:(i + 1) * blk, :], k16[:(i + 1) * blk, :])
    qk_next = qk(0)
    for i in range(nb):
        lo, hi = i * blk, (i + 1) * blk
        qk_cur, qk_next = qk_next, (qk(i + 1) if i + 1 < nb else None)
        logits2 = qk_cur * scale2
        sel = sel_s[lo:hi, :]
        extra = [logits2[:, j * blk:(j + 1) * blk] + sel[:, j:j + 1] for j in range(i)]
        o_ref[lo:hi, :] = _softmax2_pv(_causal_tile(logits2, lo, extra), v16[:hi, :]).astype(o_ref.dtype)


def _rope_tables(s_len):
    half = ROPE_DIM // 2
    inv_freq = ROPE_THETA ** (-jnp.arange(half, dtype=F32) / half)
    ang = jnp.arange(s_len, dtype=F32)[:, None] * inv_freq[None, :]
    cos, sin = jnp.cos(ang), jnp.sin(ang)
    zeros = jnp.zeros((s_len, HEAD_DIM - ROPE_DIM), F32)
    z_half = jnp.zeros((s_len, half), F32)
    cos_t = jnp.concatenate([cos, cos, zeros + 1.0], axis=1)
    sin_lo = jnp.concatenate([-sin, z_half, zeros], axis=1)
    sin_hi = jnp.concatenate([z_half, sin, zeros], axis=1)
    return cos_t, sin_lo, sin_hi


def _moba(proj, *, batch, n_heads, col_off):
    t = proj.shape[0]
    s_len = t // batch
    hd = HEAD_DIM
    seq = lambda off: pl.BlockSpec((s_len, hd), lambda b, h: (b, off + h))
    tab = pl.BlockSpec((s_len, hd), lambda b, h: (0, 0))
    return pl.pallas_call(
        _moba_kernel,
        grid=(batch, n_heads),
        in_specs=[seq(col_off), seq(col_off + n_heads), seq(col_off + 2 * n_heads), tab, tab, tab],
        out_specs=pl.BlockSpec((s_len, hd), lambda b, h: (b, h)),
        out_shape=jax.ShapeDtypeStruct((t, n_heads * hd), BF16),
        scratch_shapes=[pltpu.VMEM((s_len, hd), BF16), pltpu.VMEM((s_len, LANES), F32)],
        compiler_params=_cparams(2),
        name="moba",
    )(proj, proj, proj, *_rope_tables(s_len))


def _pad_lanes(parts):
    row = jnp.concatenate([p.reshape(1, -1).astype(F32) for p in parts], axis=1)
    return jnp.pad(row, ((0, 0), (0, LANES - row.shape[1])))


def _regroup_kernel(w_ref, main_ref, gate_ref, *, main_runs, gate_runs):
    w = w_ref[...]
    main_ref[...] = jnp.concatenate([w[:, a:b] for a, b in main_runs], axis=1).astype(main_ref.dtype)
    gate = [w[:, a:b] for a, b in gate_runs]
    n_gate = sum(b - a for a, b in gate_runs)
    gate.append(jnp.zeros((w.shape[0], gate_ref.shape[1] - n_gate), w.dtype))
    gate_ref[...] = jnp.concatenate(gate, axis=1).astype(gate_ref.dtype)


def _split_w_in(w, layer, sizes, main_idx, gate_idx, *, tr=256):
    offs = [0]
    for n in sizes:
        offs.append(offs[-1] + n)

    def runs(idx):
        out = []
        for i in idx:
            if out and out[-1][1] == offs[i]:
                out[-1] = (out[-1][0], offs[i + 1])
            else:
                out.append((offs[i], offs[i + 1]))
        return tuple(out)

    _, d, n_in = w.shape
    n_main = sum(sizes[i] for i in main_idx)
    return pl.pallas_call(
        functools.partial(_regroup_kernel, main_runs=runs(main_idx), gate_runs=runs(gate_idx)),
        grid=(d // tr,),
        in_specs=[pl.BlockSpec((None, tr, n_in), lambda i: (layer, i, 0))],
        out_specs=[pl.BlockSpec((tr, n_main), lambda i: (i, 0)), pl.BlockSpec((tr, LANES), lambda i: (i, 0))],
        out_shape=[jax.ShapeDtypeStruct((d, n_main), BF16), jax.ShapeDtypeStruct((d, LANES), BF16)],
        compiler_params=_cparams(1),
        name="regroup_w_in",
    )(w)


def kernel(x, ab_norm, ab_w_in, ab_conv_qkv, ab_a_log, ab_dt_bias, ab_out_norm, ab_lru_conv_w, ab_lru_conv_b,
           ab_lru_wa, ab_lru_ba, ab_lru_wx, ab_lru_bx, ab_lru_lambda, ab_w_out, cd_norm, cd_w_in, cd_f_bias,
           cd_w_out, ffn_norm, ffn_w_gate, ffn_w_up, ffn_w_down, final_norm):
    batch, s_len, d_model = x.shape
    depth = ffn_norm.shape[0]
    gdn_heads = ab_a_log.shape[1]
    gdn_width = gdn_heads * HEAD_DIM
    lru_width = ab_lru_lambda.shape[1]
    fox_heads = cd_f_bias.shape[1]
    fox_width = fox_heads * HEAD_DIM
    moba_width = d_model - fox_width
    moba_heads = moba_width // HEAD_DIM
    ab_sizes = (gdn_width,) * 4 + (gdn_heads,) * 2 + (lru_width,) * 2
    cd_sizes = (fox_width,) * 3 + (fox_heads,) + (moba_width,) * 3

    w_gate16, w_up16, w_down16 = (w.astype(BF16) for w in (ffn_w_gate, ffn_w_up, ffn_w_down))
    h = x.reshape(batch * s_len, d_model)
    for layer in range(depth):
        j = layer // 2
        if layer % 2 == 0:
            w_main, w_gate = _split_w_in(ab_w_in, j, ab_sizes, (0, 1, 2, 3, 6, 7), (4, 5))
            proj, gates = _inproj(h, ab_norm[j], w_main, w_gate)
            zeros_h = jnp.zeros((gdn_heads,), F32)
            o_a = _gdn(proj, gates, ab_conv_qkv[j], _pad_lanes([zeros_h, ab_a_log[j]]),
                       _pad_lanes([zeros_h, ab_dt_bias[j]]), ab_out_norm[j], batch=batch, n_heads=gdn_heads)
            o_b = _lru(proj, ab_lru_conv_w[j], ab_lru_conv_b[j], ab_lru_wa[j].astype(BF16), ab_lru_ba[j],
                       ab_lru_wx[j].astype(BF16), ab_lru_bx[j], ab_lru_lambda[j], batch=batch,
                       col_off=4 * gdn_heads)
            h = _outproj(h, o_a, o_b, ab_w_out[j].astype(BF16))
        else:
            w_main, w_gate = _split_w_in(cd_w_in, j, cd_sizes, (0, 1, 2, 4, 5, 6), (3,))
            proj, gates = _inproj(h, cd_norm[j], w_main, w_gate)
            o_c = _fox(proj, gates, _pad_lanes([cd_f_bias[j]]), batch=batch, n_heads=fox_heads)
            o_d = _moba(proj, batch=batch, n_heads=moba_heads, col_off=3 * fox_heads)
            h = _outproj(h, o_c, o_d, cd_w_out[j].astype(BF16))
        h = _ffn(h, ffn_norm[layer], w_gate16, w_up16, w_down16, final_norm, layer=layer,
                 final_norm=(layer == depth - 1))
    return h.reshape(batch, s_len, d_model)
```
